```python
import math
import functools
import jax
import jax.numpy as jnp
from jax import lax
import numpy as np

D_MODEL = 1024
BATCH = 16
SEQ = 2048
DEPTH = 2

GRID_W = 64
CTX_LEN = 256
N_EVEN = (DEPTH + 1) // 2
N_ODD = DEPTH // 2
EPS = 1e-6

GLA_HEADS = 4
GLA_DK = D_MODEL // 16
GLA_DV = D_MODEL // 8
GLA_RANK = 16
GLA_TAU = 16.0
GLA_CHUNK = 64

ML_HEADS = 4
ML_D = D_MODEL // 8
ML_CONV = 3
ML_CHUNK = 64

D_INNER = 2 * D_MODEL
SSD_HEADDIM = 64
SSD_HEADS = D_INNER // SSD_HEADDIM
SSD_GROUPS = 4
SSD_HPG = SSD_HEADS // SSD_GROUPS
SSD_STATE = 128
SSD_CONV = 3
SSD_CHUNK = 128

D_FF = 2816
FFN_CONV = 3

GLA_QK = GLA_HEADS * GLA_DK
GLA_V = GLA_HEADS * GLA_DV
ML_W = ML_HEADS * ML_D
EVEN_SIZES = (GLA_QK, GLA_QK, GLA_V, GLA_V, ML_W, ML_W, ML_W, ML_W, 4 * ML_HEADS)
EVEN_IN = sum(EVEN_SIZES)
SSD_BC = SSD_GROUPS * SSD_STATE
SSD_CONV_CH = D_INNER + 2 * SSD_BC
ODD_SIZES = (D_INNER, SSD_CONV_CH, 2 * SSD_HEADS)
ODD_IN = sum(ODD_SIZES)

kernel_name = 'hybrid_gla_mlstm_ssd_prefix_dit'


def _split(a, sizes):
    return jnp.split(a, [int(s) for s in np.cumsum(sizes)[:-1]], axis=-1)


def rmsnorm(x, g):
    xf = x.astype(jnp.float32)
    y = xf * lax.rsqrt(jnp.mean(xf * xf, axis=-1, keepdims=True) + EPS)
    return (y * g.astype(jnp.float32)).astype(x.dtype)


def modulate(x, g, shift, scale):
    return rmsnorm(x, g) * (1 + scale) + shift


def dwconv1d(x, w, b):
    k_w = w.shape[0]
    t = x.shape[1]
    pad = k_w // 2
    xp = jnp.pad(x, ((0, 0), (pad, pad), (0, 0)))
    return sum(xp[:, j:j + t] * w[j] for j in range(k_w)) + b


def dwconv2d_grid(x, w, b, rows):
    bsz, t, ch = x.shape
    y = lax.conv_general_dilated(x.reshape(bsz, rows, GRID_W, ch), w[:, :, None, :], (1, 1), 'SAME',
                                 dimension_numbers=('NHWC', 'HWIO', 'NHWC'), feature_group_count=ch)
    return y.reshape(bsz, t, ch) + b


def to_chunks(a, size):
    bsz, t = a.shape[:2]
    return jnp.moveaxis(a.reshape(bsz, t // size, size, *a.shape[2:]), 1, 0)


def from_chunks(a):
    a = jnp.moveaxis(a, 0, 1)
    return a.reshape(a.shape[0], a.shape[1] * a.shape[2], *a.shape[3:])


def gla_scan(q, k, v, log_a, s0, with_out):
    size = GLA_CHUNK
    tri = jnp.tril(jnp.ones((size, size), dtype=bool))[None, :, :, None, None]

    def step(s, inp):
        qc, kc, vc, lac = inp
        cum = jnp.cumsum(lac.astype(jnp.float32), axis=1)
        c_end = cum[:, -1]
        k_end = kc * jnp.exp(c_end[:, None] - cum)
        s_new = jnp.exp(c_end)[..., None] * s + jnp.einsum('bshk,bshv->bhkv', k_end, vc)
        if not with_out:
            return s_new, None
        rel = jnp.where(tri, cum[:, :, None] - cum[:, None, :], -jnp.inf)
        att = jnp.einsum('bthk,bshk,btshk->bhts', qc, kc, jnp.exp(rel))
        o = jnp.einsum('bhts,bshv->bthv', att, vc) + jnp.einsum('bthk,bhkv->bthv', qc * jnp.exp(cum), s)
        return s_new, o

    s, o = lax.scan(step, s0, tuple(to_chunks(a, size) for a in (q, k, v, log_a)))
    return (from_chunks(o) if with_out else None), s


def mlstm_scan(q, k, v, li, lf, state0, with_out):
    size = ML_CHUNK
    tri = jnp.tril(jnp.ones((size, size), dtype=bool))[None, :, :, None]

    def step(carry, inp):
        cmat, nvec, m = carry
        qc, kc, vc, lic, lfc = inp
        b = jnp.cumsum(lfc, axis=1)
        b_end = b[:, -1]
        w_end = b_end[:, None] - b + lic
        m_new = jnp.maximum(b_end + m, jnp.max(w_end, axis=1))
        decay = jnp.exp(b_end + m - m_new)
        ws = jnp.exp(w_end - m_new[:, None])
        c_new = decay[..., None, None] * cmat + jnp.einsum('bsh,bshk,bshv->bhkv', ws, kc, vc)
        n_new = decay[..., None] * nvec + jnp.einsum('bsh,bshk->bhk', ws, kc)
        if not with_out:
            return (c_new, n_new, m_new), None
        a_in = b + m[:, None]
        d_log = jnp.where(tri, b[:, :, None] - b[:, None, :] + lic[:, None, :], -jnp.inf)
        m_t = jnp.maximum(a_in, jnp.max(d_log, axis=2))
        dw = jnp.exp(d_log - m_t[:, :, None])
        aw = jnp.exp(a_in - m_t)
        sc = jnp.einsum('bthk,bshk->btsh', qc, kc) * dw
        num = aw[..., None] * jnp.einsum('bthk,bhkv->bthv', qc, cmat) + jnp.einsum('btsh,bshv->bthv', sc, vc)
        den = aw * jnp.einsum('bthk,bhk->bth', qc, nvec) + jnp.sum(sc, axis=2)
        h = num / jnp.maximum(jnp.abs(den), jnp.exp(-m_t))[..., None]
        return (c_new, n_new, m_new), h

    state, h = lax.scan(step, state0, tuple(to_chunks(a, size) for a in (q, k, v, li, lf)))
    return (from_chunks(h) if with_out else None), state


def ssd_scan(x, dt, bm, cm, s0, with_out, A):
    size = SSD_CHUNK
    tri = jnp.tril(jnp.ones((size, size), dtype=bool))[None, :, :, None, None]

    def step(s, inp):
        xc, dtc, bc, cc = inp
        cum = jnp.cumsum(dtc * A, axis=1)
        c_end = cum[:, -1]
        w_end = jnp.exp(c_end[:, None] - cum) * dtc
        s_new = jnp.exp(c_end)[..., None, None] * s + jnp.einsum('bsgh,bsgn,bsghp->bghpn', w_end, bc, xc)
        if not with_out:
            return s_new, None
        seg = jnp.where(tri, cum[:, :, None] - cum[:, None, :], -jnp.inf)
        cb = jnp.einsum('btgn,bsgn->btsg', cc, bc)
        w = jnp.exp(seg) * cb[..., None] * dtc[:, None]
        y = (jnp.einsum('btsgh,bsghp->btghp', w, xc)
             + jnp.exp(cum)[..., None] * jnp.einsum('btgn,bghpn->btghp', cc, s))
        return s_new, y

    s, y = lax.scan(step, s0, tuple(to_chunks(a, size) for a in (x, dt, bm, cm)))
    return (from_chunks(y) if with_out else None), s


def bidir_scan(scan_f, scan_b, ctx_f, lat_f, ctx_b, lat_b, init, ctx_out):
    def one_dir(scan_fn, ctx_args, lat_args):
        yc, s_ctx = scan_fn(*ctx_args, init, ctx_out)
        yl, _ = scan_fn(*lat_args, s_ctx, True)
        return yc, yl

    def rev(a):
        return jnp.flip(a, axis=1)

    yc_f, yl_f = one_dir(scan_f, ctx_f, lat_f)
    yc_b, yl_b = one_dir(scan_b, [rev(a) for a in ctx_b], [rev(a) for a in lat_b])
    yl = yl_f + rev(yl_b)
    yc = (yc_f + rev(yc_b)) if ctx_out else None
    return yc, yl


def even_features(h, w_in, a1, a2, ab, conv_w, conv_b, gate_b):
    bsz, t, _ = h.shape
    gq, gk, gv, gg, mq, mk, mv, mo, mg = _split(h @ w_in, EVEN_SIZES)
    log_a = [(jax.nn.log_sigmoid(((h @ a1[d]) @ a2[d] + ab[d]).astype(jnp.float32)) / GLA_TAU)
             .reshape(bsz, t, GLA_HEADS, GLA_DK) for d in range(2)]
    qk = jax.nn.silu(dwconv1d(jnp.concatenate([mq, mk], axis=-1), conv_w, conv_b))
    mq, mk = jnp.split(qk, 2, axis=-1)
    gates = (mg + gate_b).astype(jnp.float32).reshape(bsz, t, 4, ML_HEADS)
    return {
        'gla_q': gq.reshape(bsz, t, GLA_HEADS, GLA_DK) * GLA_DK ** -0.5,
        'gla_k': gk.reshape(bsz, t, GLA_HEADS, GLA_DK),
        'gla_v': gv.reshape(bsz, t, GLA_HEADS, GLA_DV),
        'gla_g': gg,
        'la_f': log_a[0], 'la_b': log_a[1],
        'ml_q': mq.reshape(bsz, t, ML_HEADS, ML_D),
        'ml_k': mk.reshape(bsz, t, ML_HEADS, ML_D) * ML_D ** -0.5,
        'ml_v': mv.reshape(bsz, t, ML_HEADS, ML_D),
        'ml_o': mo,
        'li_f': gates[:, :, 0], 'lf_f': jax.nn.log_sigmoid(gates[:, :, 1]),
        'li_b': gates[:, :, 2], 'lf_b': jax.nn.log_sigmoid(gates[:, :, 3]),
    }


def even_combine(f, o_gla, h_ml, gla_norm_g, ml_norm_g, w_out):
    bsz, t = o_gla.shape[:2]
    gla = rmsnorm(o_gla, gla_norm_g.reshape(GLA_HEADS, GLA_DV)).reshape(bsz, t, GLA_V) * jax.nn.silu(f['gla_g'])
    ml = jax.nn.sigmoid(f['ml_o']).reshape(bsz, t, ML_HEADS, ML_D) * h_ml
    ml = rmsnorm(ml, ml_norm_g.reshape(ML_HEADS, ML_D)).reshape(bsz, t, ML_W)
    return jnp.concatenate([gla, ml], axis=-1).astype(w_out.dtype) @ w_out


def even_mixer(hc, hl, w_in, a1, a2, ab, conv_w, conv_b, gate_b, gla_norm_g, ml_norm_g, w_out, ctx_out):
    fc = even_features(hc, w_in, a1, a2, ab, conv_w, conv_b, gate_b)
    fl = even_features(hl, w_in, a1, a2, ab, conv_w, conv_b, gate_b)
    bsz = hl.shape[0]

    def g_args(f, d):
        return (f['gla_q'], f['gla_k'], f['gla_v'], f['la_' + d])

    gla_init = jnp.zeros((bsz, GLA_HEADS, GLA_DK, GLA_DV), jnp.float32)
    oc_gla, ol_gla = bidir_scan(gla_scan, gla_scan, g_args(fc, 'f'), g_args(fl, 'f'),
                                g_args(fc, 'b'), g_args(fl, 'b'), gla_init, ctx_out)

    def m_args(f, d):
        return (f['ml_q'], f['ml_k'], f['ml_v'], f['li_' + d], f['lf_' + d])

    ml_init = (jnp.zeros((bsz, ML_HEADS, ML_D, ML_D), jnp.float32),
               jnp.zeros((bsz, ML_HEADS, ML_D), jnp.float32),
               jnp.zeros((bsz, ML_HEADS), jnp.float32))
    hc_ml, hl_ml = bidir_scan(mlstm_scan, mlstm_scan, m_args(fc, 'f'), m_args(fl, 'f'),
                              m_args(fc, 'b'), m_args(fl, 'b'), ml_init, ctx_out)
    yl = even_combine(fl, ol_gla, hl_ml, gla_norm_g, ml_norm_g, w_out)
    yc = even_combine(fc, oc_gla, hc_ml, gla_norm_g, ml_norm_g, w_out) if ctx_out else None
    return yc, yl


def odd_features(h, w_in, conv_w, conv_b, dt_bias):
    bsz, t, _ = h.shape
    z, xbc, dt = _split(h @ w_in, ODD_SIZES)
    xs, bm, cm = _split(jax.nn.silu(dwconv1d(xbc, conv_w, conv_b)), (D_INNER, SSD_BC, SSD_BC))
    dt = jax.nn.softplus(dt.astype(jnp.float32).reshape(bsz, t, 2, SSD_HEADS) + dt_bias.astype(jnp.float32))
    dt = dt.reshape(bsz, t, 2, SSD_GROUPS, SSD_HPG)
    return {
        'z': z,
        'x': xs.reshape(bsz, t, SSD_GROUPS, SSD_HPG, SSD_HEADDIM),
        'b': bm.reshape(bsz, t, SSD_GROUPS, SSD_STATE),
        'c': cm.reshape(bsz, t, SSD_GROUPS, SSD_STATE),
        'dt_f': dt[:, :, 0], 'dt_b': dt[:, :, 1],
    }


def odd_mixer(hc, hl, w_in, conv_w, conv_b, dt_bias, a_log, d_skip, norm_g, w_out, ctx_out):
    fc = odd_features(hc, w_in, conv_w, conv_b, dt_bias)
    fl = odd_features(hl, w_in, conv_w, conv_b, dt_bias)
    bsz = hl.shape[0]
    A = -jnp.exp(a_log.astype(jnp.float32)).reshape(2, SSD_GROUPS, SSD_HPG)
    scan_f = functools.partial(ssd_scan, A=A[0])
    scan_b = functools.partial(ssd_scan, A=A[1])

    def s_args(f, d):
        return (f['x'], f['dt_' + d], f['b'], f['c'])

    init = jnp.zeros((bsz, SSD_GROUPS, SSD_HPG, SSD_HEADDIM, SSD_STATE), jnp.float32)
    yc, yl = bidir_scan(scan_f, scan_b, s_args(fc, 'f'), s_args(fl, 'f'),
                        s_args(fc, 'b'), s_args(fl, 'b'), init, ctx_out)

    def combine(f, y):
        b_, t_ = y.shape[:2]
        y = (y + d_skip.reshape(SSD_GROUPS, SSD_HPG)[..., None] * f['x']).reshape(b_, t_, D_INNER)
        y = (y * jax.nn.silu(f['z'])).reshape(b_, t_, SSD_GROUPS, D_INNER // SSD_GROUPS)
        y = rmsnorm(y, norm_g.reshape(SSD_GROUPS, D_INNER // SSD_GROUPS)).reshape(b_, t_, D_INNER)
        return y.astype(w_out.dtype) @ w_out

    return (combine(fc, yc) if ctx_out else None), combine(fl, yl)


def conv_ffn(h, w_up, conv_w, conv_b, w_down, rows):
    u = h @ w_up
    if rows is None:
        u = dwconv1d(u, conv_w[1], conv_b)
    else:
        u = dwconv2d_grid(u, conv_w, conv_b, rows)
    a, g = jnp.split(u, 2, axis=-1)
    return (jax.nn.silu(g) * a) @ w_down


def setup_inputs(seed: int = 0) -> dict:
    key = jax.random.key(seed)
    ks = list(jax.random.split(key, 40))
    f32 = jnp.float32

    def nrm(shape, scale):
        return scale * jax.random.normal(ks.pop(), shape, f32)

    def gain(shape):
        return 1.0 + 0.05 * jax.random.normal(ks.pop(), shape, f32)

    gate_base = jnp.concatenate([jnp.zeros((ML_HEADS,), f32), jnp.linspace(3.0, 6.0, ML_HEADS, dtype=f32)] * 2)
    dt0 = jnp.exp(jax.random.uniform(ks.pop(), (N_ODD, 2, SSD_HEADS), f32, math.log(1e-3), math.log(1e-1)))
    return {
        'x': nrm((BATCH, SEQ, D_MODEL), 1.0),
        'c': nrm((BATCH, D_MODEL), 1.0),
        'ctx': nrm((BATCH, CTX_LEN, D_MODEL), 1.0),
        'c_ctx': nrm((D_MODEL,), 1.0),
        'mod_w': nrm((DEPTH, D_MODEL, 6 * D_MODEL), 0.5 * D_MODEL ** -0.5),
        'mod_b': nrm((DEPTH, 6 * D_MODEL), 0.02),
        'norm_mix_g': gain((DEPTH, D_MODEL)),
        'norm_ffn_g': gain((DEPTH, D_MODEL)),
        'final_norm_g': gain((D_MODEL,)),
        'ffn_w_up': nrm((DEPTH, D_MODEL, 2 * D_FF), D_MODEL ** -0.5),
        'ffn_conv_w': nrm((DEPTH, FFN_CONV, FFN_CONV, 2 * D_FF), 1.0 / FFN_CONV),
        'ffn_conv_b': nrm((DEPTH, 2 * D_FF), 0.02),
        'ffn_w_down': nrm((DEPTH, D_FF, D_MODEL), D_FF ** -0.5),
        'even_w_in': nrm((N_EVEN, D_MODEL, EVEN_IN), D_MODEL ** -0.5),
        'gla_a1': nrm((N_EVEN, 2, D_MODEL, GLA_RANK), D_MODEL ** -0.5),
        'gla_a2': nrm((N_EVEN, 2, GLA_RANK, GLA_QK), GLA_RANK ** -0.5),
        'gla_ab': nrm((N_EVEN, 2, GLA_QK), 0.1),
        'ml_conv_w': nrm((N_EVEN, ML_CONV, 2 * ML_W), ML_CONV ** -0.5),
        'ml_conv_b': nrm((N_EVEN, 2 * ML_W), 0.02),
        'ml_gate_b': gate_base[None] + nrm((N_EVEN, 4 * ML_HEADS), 0.1),
        'gla_norm_g': gain((N_EVEN, GLA_V)),
        'ml_norm_g': gain((N_EVEN, ML_W)),
        'even_w_out': nrm((N_EVEN, GLA_V + ML_W, D_MODEL), (GLA_V + ML_W) ** -0.5),
        'ssd_w_in': nrm((N_ODD, D_MODEL, ODD_IN), D_MODEL ** -0.5),
        'ssd_conv_w': nrm((N_ODD, SSD_CONV, SSD_CONV_CH), SSD_CONV ** -0.5),
        'ssd_conv_b': nrm((N_ODD, SSD_CONV_CH), 0.02),
        'ssd_dt_bias': dt0 + jnp.log(-jnp.expm1(-dt0)),
        'ssd_a_log': jnp.log(jax.random.uniform(ks.pop(), (N_ODD, 2, SSD_HEADS), f32, 1.0, 16.0)),
        'ssd_d': gain((N_ODD, SSD_HEADS)),
        'ssd_norm_g': gain((N_ODD, D_INNER)),
        'ssd_w_out': nrm((N_ODD, D_INNER, D_MODEL), D_INNER ** -0.5),
    }


def reference(x, c, ctx, c_ctx, mod_w, mod_b, norm_mix_g, norm_ffn_g, final_norm_g,
              ffn_w_up, ffn_conv_w, ffn_conv_b, ffn_w_down,
              even_w_in, gla_a1, gla_a2, gla_ab, ml_conv_w, ml_conv_b, ml_gate_b,
              gla_norm_g, ml_norm_g, even_w_out,
              ssd_w_in, ssd_conv_w, ssd_conv_b, ssd_dt_bias, ssd_a_log, ssd_d, ssd_norm_g, ssd_w_out):
    rows = x.shape[1] // GRID_W
    xl, xc = x, ctx
    for layer in range(DEPTH):
        last = layer == DEPTH - 1
        mod_l = (jax.nn.silu(c) @ mod_w[layer] + mod_b[layer])[:, None, :]
        mod_c = (jax.nn.silu(c_ctx) @ mod_w[layer] + mod_b[layer])[None, None, :]
        sh1l, sc1l, g1l, sh2l, sc2l, g2l = jnp.split(mod_l, 6, axis=-1)
        sh1c, sc1c, g1c, sh2c, sc2c, g2c = jnp.split(mod_c, 6, axis=-1)
        hl = modulate(xl, norm_mix_g[layer], sh1l, sc1l)
        hc = modulate(xc, norm_mix_g[layer], sh1c, sc1c)
        if layer % 2 == 0:
            e = layer // 2
            yc, yl = even_mixer(hc, hl, even_w_in[e], gla_a1[e], gla_a2[e], gla_ab[e], ml_conv_w[e], ml_conv_b[e],
                                ml_gate_b[e], gla_norm_g[e], ml_norm_g[e], even_w_out[e], not last)
        else:
            o = layer // 2
            yc, yl = odd_mixer(hc, hl, ssd_w_in[o], ssd_conv_w[o], ssd_conv_b[o], ssd_dt_bias[o], ssd_a_log[o],
                               ssd_d[o], ssd_norm_g[o], ssd_w_out[o], not last)
        xl = xl + g1l * yl
        hl = modulate(xl, norm_ffn_g[layer], sh2l, sc2l)
        xl = xl + g2l * conv_ffn(hl, ffn_w_up[layer], ffn_conv_w[layer], ffn_conv_b[layer], ffn_w_down[layer], rows)
        if not last:
            xc = xc + g1c * yc
            hc = modulate(xc, norm_ffn_g[layer], sh2c, sc2c)
            xc = xc + g2c * conv_ffn(hc, ffn_w_up[layer], ffn_conv_w[layer], ffn_conv_b[layer], ffn_w_down[layer], None)
    return rmsnorm(xl, final_norm_g)
```

```python
import functools

import numpy as np
import jax
import jax.numpy as jnp
from jax import lax
from jax.experimental import pallas as pl
from jax.experimental.pallas import tpu as pltpu

F32 = jnp.float32
BF16 = jnp.bfloat16

D_MODEL = 1024
GRID_W = 64
EPS = 1e-6

GLA_HEADS, GLA_DK, GLA_DV, GLA_RANK, GLA_TAU, GLA_CHUNK = 4, 64, 128, 16, 16.0, 64
ML_HEADS, ML_D, ML_CHUNK = 4, 128, 64
D_INNER, SSD_HEADS, SSD_GROUPS, SSD_HPG, SSD_P, SSD_STATE, SSD_CHUNK = 2048, 32, 4, 8, 64, 128, 128
D_FF = 2816
GLA_QK, GLA_V, ML_W = GLA_HEADS * GLA_DK, GLA_HEADS * GLA_DV, ML_HEADS * ML_D
EVEN_IN = 3600
ODD_IN = 5184
SSD_BC = SSD_GROUPS * SSD_STATE
SSD_CONV_CH = D_INNER + 2 * SSD_BC

LANES = 128
EVEN_N = 3712
ODD_N = 5248
V7X_VMEM_BYTES = 64 * 1024 * 1024
VMEM_LIMIT = V7X_VMEM_BYTES - 8 * 1024 * 1024

NEG_INF = float("-inf")


def _params(*sem):
    return pltpu.CompilerParams(dimension_semantics=sem, vmem_limit_bytes=VMEM_LIMIT)


def _sigmoid(x):
    return 1.0 / (1.0 + jnp.exp(-x))


def _silu(x):
    return x * _sigmoid(x)


def _softplus(x):
    return jnp.maximum(x, 0.0) + jnp.log(1.0 + jnp.exp(-jnp.abs(x)))


def _log_sigmoid(x):
    return -_softplus(-x)


def _hi_lo(x):
    hi = x.astype(BF16)
    lo = (x - hi.astype(F32)).astype(BF16)
    return hi, lo


def _dot(a, b):
    return jnp.dot(a.astype(BF16), b.astype(BF16), preferred_element_type=F32)


def _dot_nt(a, b):
    return lax.dot_general(a.astype(BF16), b.astype(BF16), (((1,), (1,)), ((), ())),
                           preferred_element_type=F32)


def _dot_tn(a, b):
    return lax.dot_general(a.astype(BF16), b.astype(BF16), (((0,), (0,)), ((), ())),
                           preferred_element_type=F32)


def _norm_mod(x, g, sc, sh):
    y = x * lax.rsqrt(jnp.mean(x * x, axis=-1, keepdims=True) + EPS)
    return (y * g) * (1.0 + sc) + sh


def _mod_kernel(c_ref, w_ref, b_ref, o_ref):
    s = _silu(c_ref[...])
    hi, lo = _hi_lo(s)
    w = w_ref[0]
    whi, wlo = _hi_lo(w)
    acc = jnp.dot(hi, whi, preferred_element_type=F32)
    acc += jnp.dot(lo, whi, preferred_element_type=F32)
    acc += jnp.dot(hi, wlo, preferred_element_type=F32)
    o_ref[0] = acc + b_ref[0]


def _modulation(cc, mod_w, mod_b):
    depth, d, n = mod_w.shape
    rows = cc.shape[0]
    tn = 1536
    return pl.pallas_call(
        _mod_kernel,
        grid=(depth, n // tn),
        in_specs=[pl.BlockSpec((rows, d), lambda l, j: (0, 0)),
                  pl.BlockSpec((1, d, tn), lambda l, j: (l, 0, j)),
                  pl.BlockSpec((1, 1, tn), lambda l, j: (l, 0, j))],
        out_specs=pl.BlockSpec((1, rows, tn), lambda l, j: (l, 0, j)),
        out_shape=jax.ShapeDtypeStruct((depth, rows, n), F32),
        compiler_params=_params("parallel", "parallel"),
        name="modulation",
    )(cc, mod_w, mod_b.reshape(depth, 1, n))


def _norm_proj_kernel(x_ref, g_ref, sc_ref, sh_ref, w_ref, o_ref):
    hb = _norm_mod(x_ref[0], g_ref[...], sc_ref[0], sh_ref[0]).astype(BF16)
    n = w_ref.shape[1]
    for s in range(0, n, 1024):
        e = min(s + 1024, n)
        o_ref[0, :, s:e] = jnp.dot(hb, w_ref[:, s:e], preferred_element_type=F32)


def _bvec_spec(v):
    d = v.shape[-1]
    if v.shape[0] == 1:
        return pl.BlockSpec((1, 1, d), lambda b, t: (0, 0, 0))
    return pl.BlockSpec((1, 1, d), lambda b, t: (b, 0, 0))


def _norm_proj(x, g, sc, sh, w, tm):
    bsz, t, d = x.shape
    n = w.shape[1]
    return pl.pallas_call(
        _norm_proj_kernel,
        grid=(bsz, t // tm),
        in_specs=[pl.BlockSpec((1, tm, d), lambda b, i: (b, i, 0)),
                  pl.BlockSpec((1, d), lambda b, i: (0, 0)),
                  _bvec_spec(sc), _bvec_spec(sh),
                  pl.BlockSpec((d, n), lambda b, i: (0, 0))],
        out_specs=pl.BlockSpec((1, tm, n), lambda b, i: (b, i, 0)),
        out_shape=jax.ShapeDtypeStruct((bsz, t, n), F32),
        compiler_params=_params("parallel", "parallel"),
        name="norm_proj",
    )(x, g.reshape(1, d), sc, sh, w)


def _shift_prev(u, first_mask):
    return jnp.where(first_mask, 0.0, pltpu.roll(u, 1, 0))


def _shift_next(u, last_mask):
    return jnp.where(last_mask, 0.0, pltpu.roll(u, u.shape[0] - 1, 0))


def _dwconv_silu_kernel(x_ref, w_ref, b_ref, o_ref):
    x = x_ref[0]
    t = x.shape[0]
    tok = lax.broadcasted_iota(jnp.int32, x.shape, 0)
    y = (_shift_prev(x, tok == 0) * w_ref[0:1] + x * w_ref[1:2]
         + _shift_next(x, tok == t - 1) * w_ref[2:3] + b_ref[...])
    o_ref[0] = _silu(y)


def _dwconv_silu(x, col0, w, b):
    bsz, t, _ = x.shape
    ch = w.shape[1]
    ct = 256
    off = col0 // ct
    return pl.pallas_call(
        _dwconv_silu_kernel,
        grid=(bsz, ch // ct),
        in_specs=[pl.BlockSpec((1, t, ct), lambda b_, j: (b_, 0, off + j)),
                  pl.BlockSpec((3, ct), lambda b_, j: (0, j)),
                  pl.BlockSpec((1, ct), lambda b_, j: (0, j))],
        out_specs=pl.BlockSpec((1, t, ct), lambda b_, j: (b_, 0, j)),
        out_shape=jax.ShapeDtypeStruct((bsz, t, ch), F32),
        compiler_params=_params("parallel", "parallel"),
        name="dwconv_silu",
    )(x, w, b.reshape(1, ch))


def _flip2(a):
    return a[..., ::-1, ::-1].copy()


GLA_LEVELS = (32, 16, 8, 4, 2, 1)
GLA_U_ROWS = 14 * GLA_CHUNK + 8


def _gla_constants(reverse):
    L = GLA_CHUNK
    t = np.arange(L)[:, None]
    j = np.arange(L)[None, :]
    blocks = [(j <= t), (j > t)]
    uq, uk, masks = [], [], [np.eye(L)]
    for m in GLA_LEVELS:
        pos, blk = t % (2 * m), t // (2 * m)
        ref = blk * 2 * m + m - 1
        uq.append((pos >= m) & (j > ref) & (j <= t))
        uk.append((pos < m) & (j > t) & (j <= ref))
        masks.append((blk == blk.T) & (pos >= m) & (pos.T < m))
    blocks = [np.asarray(b_, np.float32) for b_ in blocks + uq + uk]
    masks = [np.asarray(m_, np.float32) for m_ in masks]
    if reverse:
        blocks = [_flip2(b_) for b_ in blocks]
        masks = [_flip2(m_) for m_ in masks]
    u = np.concatenate(blocks + [np.ones((8, L), np.float32)], axis=0)
    mask2 = np.stack([np.concatenate([m_, m_], axis=0) for m_ in masks])
    return jnp.asarray(u, BF16), jnp.asarray(mask2, F32)


def _tri(L, reverse):
    t = np.arange(L)[:, None]
    j = np.arange(L)[None, :]
    return np.asarray((j >= t) if reverse else (j <= t), np.float32)


def _chunk_index(n_chunks, reverse):
    if reverse:
        return lambda c: n_chunks - 1 - c
    return lambda c: c


def _gla_kernel(q_ref, k_ref, v_ref, lr_ref, a2_ref, ab_ref, u_ref, mask_ref, s0_ref, *rest, n_chunks):
    prev_ref = rest[0] if len(rest) == 4 else None
    o_ref, s_out_ref, s_scr = rest[-3:]
    c = pl.program_id(1)
    L = GLA_CHUNK

    @pl.when(c == 0)
    def _():
        s_scr[...] = s0_ref[0]

    la = _log_sigmoid(_dot(lr_ref[0], a2_ref[...]) + ab_ref[...]) * (1.0 / GLA_TAU)
    hi, lo = _hi_lo(la)
    e2 = jnp.dot(u_ref[...], jnp.concatenate([hi, lo], axis=1), preferred_element_type=F32)
    e = e2[:, :GLA_QK] + e2[:, GLA_QK:]
    cum, e_end, c_end = e[0:L], e[L:2 * L], e[14 * L:14 * L + 1]
    q = q_ref[0] * (GLA_DK ** -0.5)
    k = k_ref[0]
    v = v_ref[0]
    qs = [q] + [q * jnp.exp(e[(2 + l) * L:(3 + l) * L]) for l in range(6)]
    ks = [k] + [k * jnp.exp(e[(8 + l) * L:(9 + l) * L]) for l in range(6)]
    q_in = q * jnp.exp(cum)
    k_end = k * jnp.exp(e_end)
    lane = lax.broadcasted_iota(jnp.int32, (L, LANES), 1)
    first = lane < GLA_DK
    for j in range(2):
        sl = slice(j * LANES, (j + 1) * LANES)
        att = jnp.zeros((2 * L, L), F32)
        for lvl in range(7):
            ql = qs[lvl][:, sl]
            lhs = jnp.concatenate([jnp.where(first, ql, 0.0), jnp.where(first, 0.0, ql)], axis=0)
            att = att + _dot_nt(lhs, ks[lvl][:, sl]) * mask_ref[lvl]
        vj = v[:, 2 * j * GLA_DV:(2 * j + 2) * GLA_DV]
        st = s_scr[j]
        o_a = _dot(att[:L], vj[:, :GLA_DV])
        o_b = _dot(att[L:], vj[:, GLA_DV:])
        o = jnp.concatenate([o_a, o_b], axis=1) + _dot_nt(q_in[:, sl], st)
        vsl = slice(2 * j * GLA_DV, (2 * j + 2) * GLA_DV)
        o_ref[0, :, vsl] = o if prev_ref is None else o + prev_ref[0, :, vsl]
        upd = _dot_tn(vj, k_end[:, sl])
        row = lax.broadcasted_iota(jnp.int32, upd.shape, 0)
        lane2 = lax.broadcasted_iota(jnp.int32, upd.shape, 1)
        same_head = (row < GLA_DV) == (lane2 < GLA_DK)
        s_scr[j] = st * jnp.exp(c_end[:, sl]) + jnp.where(same_head, upd, 0.0)

    @pl.when(c == n_chunks - 1)
    def _():
        s_out_ref[0] = s_scr[...]


def _gla_scan(proj, a2p, ab, s0, prev, reverse):
    bsz, t, _ = proj.shape
    L = GLA_CHUNK
    nc = t // L
    ci = _chunk_index(nc, reverse)
    u, mask2 = _gla_constants(reverse)
    col = lambda w_, i: pl.BlockSpec((1, L, w_), lambda b, c: (b, ci(c), i))
    const = lambda a: pl.BlockSpec(a.shape, lambda b, c: (0,) * a.ndim)
    st_spec = pl.BlockSpec((1, 2, 2 * GLA_DV, LANES), lambda b, c: (b, 0, 0, 0))
    in_specs = [col(GLA_QK, 0), col(GLA_QK, 1), col(GLA_V, 1), col(LANES, 28),
                const(a2p), const(ab), const(u), const(mask2), st_spec]
    args = [proj, proj, proj, proj, a2p, ab, u, mask2, s0]
    if prev is not None:
        in_specs.append(col(GLA_V, 0))
        args.append(prev)
    return pl.pallas_call(
        functools.partial(_gla_kernel, n_chunks=nc),
        grid=(bsz, nc),
        in_specs=in_specs,
        out_specs=[col(GLA_V, 0), st_spec],
        out_shape=[jax.ShapeDtypeStruct((bsz, t, GLA_V), F32),
                   jax.ShapeDtypeStruct((bsz, 2, 2 * GLA_DV, LANES), F32)],
        scratch_shapes=[pltpu.VMEM((2, 2 * GLA_DV, LANES), F32)],
        compiler_params=_params("parallel", "arbitrary"),
        name="gla_scan",
    )(*args)


def _mlstm_kernel(qk_ref, kk_ref, v_ref, gc_ref, gr_ref, gbc_ref, gbr_ref, tri_ref, trit_ref,
                  c0_ref, n0_ref, m0_ref, *rest, n_chunks, reverse):
    prev_ref = rest[0] if len(rest) == 8 else None
    o_ref, c_out_ref, n_out_ref, m_out_ref, c_scr, n_scr, m_scr = rest[-7:]
    c = pl.program_id(1)
    L = ML_CHUNK
    direction = 1 if reverse else 0
    end = 0 if reverse else L - 1

    @pl.when(c == 0)
    def _():
        c_scr[...] = c0_ref[0]
        n_scr[...] = n0_ref[0]
        m_scr[...] = m0_ref[0]

    gates_c = gc_ref[0] + gbc_ref[...]
    gates_r = gr_ref[0, 0] + gbr_ref[...]
    lf_c = _log_sigmoid(gates_c)
    lf_r = _log_sigmoid(gates_r)
    hi, lo = _hi_lo(lf_c)
    b2 = jnp.dot(tri_ref[...], jnp.concatenate([hi, lo], axis=1), preferred_element_type=F32)
    b_cols = b2[:, :LANES] + b2[:, LANES:]
    hi, lo = _hi_lo(lf_r)
    b2 = jnp.dot(jnp.concatenate([hi, lo], axis=0), trit_ref[...], preferred_element_type=F32)
    b_rows = b2[:16] + b2[16:]
    tri = tri_ref[...] > 0.5
    for h in range(ML_HEADS):
        sl = slice(h * ML_D, (h + 1) * ML_D)
        ci_, cf_ = 8 * direction + h, 8 * direction + 4 + h
        b_c, b_r = b_cols[:, cf_:cf_ + 1], b_rows[cf_:cf_ + 1, :]
        li_c, li_r = gates_c[:, ci_:ci_ + 1], gates_r[ci_:ci_ + 1, :]
        b_end = b_r[:, end:end + 1]
        m = m_scr[h][:, 0:1]
        cmat = c_scr[h]
        nvec = n_scr[h]
        q = qk_ref[0][:, sl]
        k = kk_ref[0][:, sl] * (ML_D ** -0.5)
        v = v_ref[0][:, sl]
        a_in = b_c + m
        d_log = jnp.where(tri, b_c - b_r + li_r, NEG_INF)
        m_t = jnp.maximum(a_in, jnp.max(d_log, axis=1, keepdims=True))
        dw = jnp.exp(d_log - m_t)
        aw = jnp.exp(a_in - m_t)
        sc = _dot_nt(q, k) * dw
        num = aw * _dot(q, cmat) + _dot(sc, v)
        den = aw * jnp.sum(q * nvec, axis=1, keepdims=True) + jnp.sum(sc, axis=1, keepdims=True)
        hout = num / jnp.maximum(jnp.abs(den), jnp.exp(-m_t))
        o_ref[0, :, sl] = hout if prev_ref is None else hout + prev_ref[0, :, sl]
        w_end = b_end - b_r + li_r
        m_new = jnp.maximum(b_end + m, jnp.max(w_end, axis=1, keepdims=True))
        decay = jnp.exp(b_end + m - m_new)
        kw = k * jnp.exp(b_end - b_c + li_c - m_new)
        c_scr[h] = decay * cmat + _dot_tn(kw, v)
        n_scr[h] = decay * nvec + jnp.sum(kw, axis=0, keepdims=True)
        m_scr[h] = jnp.broadcast_to(m_new, (1, LANES))

    @pl.when(c == n_chunks - 1)
    def _():
        c_out_ref[0] = c_scr[...]
        n_out_ref[0] = n_scr[...]
        m_out_ref[0] = m_scr[...]


def _mlstm_scan(qk, proj, gates_r, gate_b, state, prev, reverse):
    bsz, t, _ = proj.shape
    L = ML_CHUNK
    nc = t // L
    ci = _chunk_index(nc, reverse)
    tri = _tri(L, reverse)
    tri_b, trit_b = jnp.asarray(tri, BF16), jnp.asarray(tri.T, BF16)
    gbc = jnp.zeros((1, LANES), F32).at[0, :16].set(gate_b)
    gbr = gate_b.reshape(16, 1)
    c0, n0, m0 = state
    const = lambda a: pl.BlockSpec(a.shape, lambda b, c: (0,) * a.ndim)
    seq = lambda w_, i: pl.BlockSpec((1, L, w_), lambda b, c: (b, ci(c), i))
    st = lambda a: pl.BlockSpec((1,) + a.shape[1:], lambda b, c: (b,) + (0,) * (a.ndim - 1))
    in_specs = [seq(ML_W, 0), seq(ML_W, 1), seq(ML_W, 5), seq(LANES, 28),
                pl.BlockSpec((1, 1, 16, L), lambda b, c: (b, ci(c), 0, 0)),
                const(gbc), const(gbr), const(tri_b), const(trit_b), st(c0), st(n0), st(m0)]
    args = [qk, qk, proj, proj, gates_r, gbc, gbr, tri_b, trit_b, c0, n0, m0]
    if prev is not None:
        in_specs.append(seq(ML_W, 0))
        args.append(prev)
    outs = pl.pallas_call(
        functools.partial(_mlstm_kernel, n_chunks=nc, reverse=reverse),
        grid=(bsz, nc),
        in_specs=in_specs,
        out_specs=[seq(ML_W, 0), st(c0), st(n0), st(m0)],
        out_shape=[jax.ShapeDtypeStruct((bsz, t, ML_W), F32),
                   jax.ShapeDtypeStruct(c0.shape, F32), jax.ShapeDtypeStruct(n0.shape, F32),
                   jax.ShapeDtypeStruct(m0.shape, F32)],
        scratch_shapes=[pltpu.VMEM(c0.shape[1:], F32), pltpu.VMEM(n0.shape[1:], F32),
                        pltpu.VMEM(m0.shape[1:], F32)],
        compiler_params=_params("parallel", "arbitrary"),
        name="mlstm_scan",
    )(*args)
    return outs[0], tuple(outs[1:])


def _ssd_kernel(x_ref, bm_ref, cm_ref, dtc_ref, dtr_ref, bias_c_ref, bias_r_ref, a_c_ref, a_r_ref,
                tri_ref, trit_ref, s0_ref, *rest, n_chunks, reverse, with_out, has_prev):
    prev_ref = rest[0] if has_prev else None
    y_ref = rest[-3] if with_out else None
    s_out_ref, s_scr = rest[-2:]
    c = pl.program_id(1)
    L = SSD_CHUNK
    direction = 1 if reverse else 0
    end = 0 if reverse else L - 1

    @pl.when(c == 0)
    def _():
        s_scr[...] = s0_ref[0]

    dt_c = _softplus(dtc_ref[0] + bias_c_ref[...])
    dt_r = _softplus(dtr_ref[0, 0] + bias_r_ref[...])
    hi, lo = _hi_lo(dt_c * -jnp.exp(a_c_ref[...]))
    c2 = jnp.dot(tri_ref[...], jnp.concatenate([hi, lo], axis=1), preferred_element_type=F32)
    cum_c = c2[:, :LANES] + c2[:, LANES:]
    hi, lo = _hi_lo(dt_r * -jnp.exp(a_r_ref[...]))
    c2 = jnp.dot(jnp.concatenate([hi, lo], axis=0), trit_ref[...], preferred_element_type=F32)
    cum_r = c2[:SSD_HEADS * 2] + c2[SSD_HEADS * 2:]
    tri = tri_ref[...] > 0.5
    lane = lax.broadcasted_iota(jnp.int32, (L, LANES), 1)
    first = lane < SSD_P
    for g in range(SSD_GROUPS):
        bg = bm_ref[0][:, g * SSD_STATE:(g + 1) * SSD_STATE]
        cg = cm_ref[0][:, g * SSD_STATE:(g + 1) * SSD_STATE]
        gsl = slice(g * SSD_HPG * SSD_P, (g + 1) * SSD_HPG * SSD_P)
        sg = s_scr[g]
        if with_out:
            cb = _dot_nt(cg, bg)
            y_inter = _dot(cg, sg)
        xw_tiles, dec_tiles = [], []
        for j in range(SSD_HPG // 2):
            ha = direction * SSD_HEADS + g * SSD_HPG + 2 * j
            hb = ha + 1
            tile = slice(g * SSD_HPG * SSD_P + j * LANES, g * SSD_HPG * SSD_P + (j + 1) * LANES)
            xt = x_ref[0][:, tile]
            cum_a, cum_b = cum_c[:, ha:ha + 1], cum_c[:, hb:hb + 1]
            end_a, end_b = cum_a[end:end + 1], cum_b[end:end + 1]
            if with_out:
                ws = []
                for hh, cum_h in ((ha, cum_a), (hb, cum_b)):
                    seg = jnp.exp(jnp.where(tri, cum_h - cum_r[hh:hh + 1, :], NEG_INF))
                    ws.append(seg * cb * dt_r[hh:hh + 1, :])
                w2 = jnp.concatenate(ws, axis=1)
                x2 = jnp.concatenate([jnp.where(first, xt, 0.0), jnp.where(first, 0.0, xt)], axis=0)
                e_cum = jnp.where(first, jnp.exp(cum_a), jnp.exp(cum_b))
                y = _dot(w2, x2) + e_cum * y_inter[:, j * LANES:(j + 1) * LANES]
                y_ref[0, :, tile] = y if prev_ref is None else y + prev_ref[0, :, tile]
            w_end = jnp.where(first, jnp.exp(end_a - cum_a) * dt_c[:, ha:ha + 1],
                              jnp.exp(end_b - cum_b) * dt_c[:, hb:hb + 1])
            xw_tiles.append(xt * w_end)
            dec_tiles.append(jnp.where(first[0:1], jnp.exp(end_a), jnp.exp(end_b)))
        xw = jnp.concatenate(xw_tiles, axis=1)
        dec = jnp.concatenate(dec_tiles, axis=1)
        s_scr[g] = dec * sg + _dot_tn(bg, xw)

    @pl.when(c == n_chunks - 1)
    def _():
        s_out_ref[0] = s_scr[...]


def _ssd_scan(xbc, proj, dt_r, dt_bias, a_log, s0, prev, reverse, with_out):
    bsz, t, _ = xbc.shape
    L = SSD_CHUNK
    nc = t // L
    ci = _chunk_index(nc, reverse)
    tri = _tri(L, reverse)
    tri_b, trit_b = jnp.asarray(tri, BF16), jnp.asarray(tri.T, BF16)
    nh = 2 * SSD_HEADS
    bias_c = jnp.zeros((1, LANES), F32).at[0, :nh].set(dt_bias.reshape(nh))
    a_c = jnp.zeros((1, LANES), F32).at[0, :nh].set(a_log.reshape(nh))
    bias_r, a_r = dt_bias.reshape(nh, 1), a_log.reshape(nh, 1)
    const = lambda a: pl.BlockSpec(a.shape, lambda b, c: (0,) * a.ndim)
    seq = lambda w_, i: pl.BlockSpec((1, L, w_), lambda b, c: (b, ci(c), i))
    st_spec = pl.BlockSpec((1,) + s0.shape[1:], lambda b, c: (b, 0, 0, 0))
    in_specs = [seq(D_INNER, 0), seq(SSD_BC, D_INNER // SSD_BC), seq(SSD_BC, D_INNER // SSD_BC + 1),
                seq(LANES, ODD_IN // LANES),
                pl.BlockSpec((1, 1, nh, L), lambda b, c: (b, ci(c), 0, 0)),
                const(bias_c), const(bias_r), const(a_c), const(a_r), const(tri_b), const(trit_b), st_spec]
    args = [xbc, xbc, xbc, proj, dt_r, bias_c, bias_r, a_c, a_r, tri_b, trit_b, s0]
    out_specs, out_shape = [st_spec], [jax.ShapeDtypeStruct(s0.shape, F32)]
    if prev is not None:
        in_specs.append(seq(D_INNER, 0))
        args.append(prev)
    if with_out:
        out_specs = [seq(D_INNER, 0)] + out_specs
        out_shape = [jax.ShapeDtypeStruct((bsz, t, D_INNER), F32)] + out_shape
    outs = pl.pallas_call(
        functools.partial(_ssd_kernel, n_chunks=nc, reverse=reverse, with_out=with_out,
                          has_prev=prev is not None),
        grid=(bsz, nc), in_specs=in_specs, out_specs=out_specs, out_shape=out_shape,
        scratch_shapes=[pltpu.VMEM(s0.shape[1:], F32)],
        compiler_params=_params("parallel", "arbitrary"),
        name="ssd_scan",
    )(*args)
    return (outs[0], outs[1]) if with_out else (None, outs[0])


def _head_rms(y, g, width):
    parts = []
    for s in range(0, y.shape[1], width):
        p = y[:, s:s + width]
        parts.append(p * lax.rsqrt(jnp.mean(p * p, axis=-1, keepdims=True) + EPS))
    return jnp.concatenate(parts, axis=1) * g


def _even_out_kernel(og_ref, hm_ref, gg_ref, mo_ref, gn_ref, mn_ref, w_ref, x_ref, gate_ref, o_ref):
    gla = _head_rms(og_ref[0], gn_ref[...], GLA_DV) * _silu(gg_ref[0])
    ml = _head_rms(_sigmoid(mo_ref[0]) * hm_ref[0], mn_ref[...], ML_D)
    y = _dot(gla, w_ref[:GLA_V]) + _dot(ml, w_ref[GLA_V:])
    o_ref[0] = x_ref[0] + gate_ref[0] * y


def _even_out(o_gla, h_ml, proj, gla_norm_g, ml_norm_g, w_out, x, gate, tm):
    bsz, t, d = x.shape
    tok = lambda w_, i: pl.BlockSpec((1, tm, w_), lambda b, s: (b, s, i))
    const = lambda a: pl.BlockSpec(a.shape, lambda b, s: (0,) * a.ndim)
    gn, mn = gla_norm_g.reshape(1, GLA_V), ml_norm_g.reshape(1, ML_W)
    return pl.pallas_call(
        _even_out_kernel,
        grid=(bsz, t // tm),
        in_specs=[tok(GLA_V, 0), tok(ML_W, 0), tok(GLA_V, 2), tok(ML_W, 6), const(gn), const(mn),
                  const(w_out), tok(d, 0), _bvec_spec(gate)],
        out_specs=tok(d, 0),
        out_shape=jax.ShapeDtypeStruct(x.shape, F32),
        compiler_params=_params("parallel", "parallel"),
        name="even_out",
    )(o_gla, h_ml, proj, proj, gn, mn, w_out, x, gate)


def _odd_out_kernel(y_ref, xs_ref, z_ref, dsk_ref, ng_ref, w_ref, x_ref, gate_ref, o_ref):
    y = (y_ref[0] + dsk_ref[...] * xs_ref[0]) * _silu(z_ref[0])
    y = _head_rms(y, ng_ref[...], D_INNER // SSD_GROUPS)
    o_ref[0] = x_ref[0] + gate_ref[0] * _dot(y, w_ref[...])


def _odd_out(y, xbc, proj, d_skip, norm_g, w_out, x, gate, tm):
    bsz, t, d = x.shape
    tok = lambda w_, i: pl.BlockSpec((1, tm, w_), lambda b, s: (b, s, i))
    const = lambda a: pl.BlockSpec(a.shape, lambda b, s: (0,) * a.ndim)
    dsk = jnp.repeat(d_skip, SSD_P).reshape(1, D_INNER)
    ng = norm_g.reshape(1, D_INNER)
    return pl.pallas_call(
        _odd_out_kernel,
        grid=(bsz, t // tm),
        in_specs=[tok(D_INNER, 0), tok(D_INNER, 0), tok(D_INNER, 0), const(dsk), const(ng),
                  const(w_out), tok(d, 0), _bvec_spec(gate)],
        out_specs=tok(d, 0),
        out_shape=jax.ShapeDtypeStruct(x.shape, F32),
        compiler_params=_params("parallel", "parallel"),
        name="odd_out",
    )(y, xbc, proj, dsk, ng, w_out, x, gate)


def _ffn_conv(u, w9, b, grid_conv):
    t = u.shape[0]
    tok = lax.broadcasted_iota(jnp.int32, u.shape, 0)
    if grid_conv:
        col = tok % GRID_W
        ul = _shift_prev(u, col == 0)
        ur = _shift_next(u, col == GRID_W - 1)
    else:
        ul = _shift_prev(u, tok == 0)
        ur = _shift_next(u, tok == t - 1)
    out = ul * w9[3:4] + u * w9[4:5] + ur * w9[5:6] + b
    if grid_conv:
        top = ul * w9[0:1] + u * w9[1:2] + ur * w9[2:3]
        bot = ul * w9[6:7] + u * w9[7:8] + ur * w9[8:9]
        pad = jnp.zeros((GRID_W, u.shape[1]), F32)
        out = out + jnp.concatenate([pad, top[:t - GRID_W]], axis=0) + jnp.concatenate([bot[GRID_W:], pad], axis=0)
    return out


def _ffn_up_kernel(x_ref, g_ref, sc_ref, sh_ref, wa_ref, wg_ref, cwa_ref, cwg_ref, cba_ref, cbg_ref,
                   o_ref, h_scr, *, grid_conv):
    @pl.when(pl.program_id(1) == 0)
    def _():
        h_scr[...] = _norm_mod(x_ref[0], g_ref[...], sc_ref[0], sh_ref[0]).astype(BF16)

    hb = h_scr[...]
    for s in range(0, o_ref.shape[2], LANES):
        sl = slice(s, s + LANES)
        ua = jnp.dot(hb, wa_ref[:, sl], preferred_element_type=F32)
        ug = jnp.dot(hb, wg_ref[:, sl], preferred_element_type=F32)
        a = _ffn_conv(ua, cwa_ref[:, sl], cba_ref[:, sl], grid_conv)
        g = _ffn_conv(ug, cwg_ref[:, sl], cbg_ref[:, sl], grid_conv)
        o_ref[0, :, sl] = (_silu(g) * a).astype(BF16)


def _ffn_up(x, g, sc, sh, w_up, conv_w9, conv_b, grid_conv):
    bsz, t, d = x.shape
    ct = 256
    nct = D_FF // ct
    cb = conv_b.reshape(1, 2 * D_FF)
    return pl.pallas_call(
        functools.partial(_ffn_up_kernel, grid_conv=grid_conv),
        grid=(bsz, nct),
        in_specs=[pl.BlockSpec((1, t, d), lambda b, j: (b, 0, 0)),
                  pl.BlockSpec((1, d), lambda b, j: (0, 0)),
                  _bvec_spec(sc), _bvec_spec(sh),
                  pl.BlockSpec((d, ct), lambda b, j: (0, j)),
                  pl.BlockSpec((d, ct), lambda b, j: (0, nct + j)),
                  pl.BlockSpec((9, ct), lambda b, j: (0, j)),
                  pl.BlockSpec((9, ct), lambda b, j: (0, nct + j)),
                  pl.BlockSpec((1, ct), lambda b, j: (0, j)),
                  pl.BlockSpec((1, ct), lambda b, j: (0, nct + j))],
        out_specs=pl.BlockSpec((1, t, ct), lambda b, j: (b, 0, j)),
        out_shape=jax.ShapeDtypeStruct((bsz, t, D_FF), BF16),
        scratch_shapes=[pltpu.VMEM((t, d), BF16)],
        compiler_params=_params("parallel", "arbitrary"),
        name="ffn_up",
    )(x, g.reshape(1, d), sc, sh, w_up, w_up, conv_w9, conv_w9, cb, cb)


def _ffn_down_kernel(a_ref, w_ref, x_ref, gate_ref, fg_ref, o_ref, *, final_norm):
    y = x_ref[0] + gate_ref[0] * jnp.dot(a_ref[0], w_ref[...], preferred_element_type=F32)
    if final_norm:
        y = y * lax.rsqrt(jnp.mean(y * y, axis=-1, keepdims=True) + EPS) * fg_ref[...]
    o_ref[0] = y


def _ffn_down(act, w_down, x, gate, final_g, final_norm, tm):
    bsz, t, d = x.shape
    tok = lambda w_: pl.BlockSpec((1, tm, w_), lambda b, s: (b, s, 0))
    const = lambda a: pl.BlockSpec(a.shape, lambda b, s: (0,) * a.ndim)
    fg = final_g.reshape(1, d)
    return pl.pallas_call(
        functools.partial(_ffn_down_kernel, final_norm=final_norm),
        grid=(bsz, t // tm),
        in_specs=[tok(D_FF), const(w_down), tok(d), _bvec_spec(gate), const(fg)],
        out_specs=tok(d),
        out_shape=jax.ShapeDtypeStruct(x.shape, F32),
        compiler_params=_params("parallel", "parallel"),
        name="ffn_down",
    )(act, w_down, x, gate, fg)


def _rows_to_chunks(a, L):
    bsz, t, ch = a.shape
    return a.reshape(bsz, t // L, L, ch).transpose(0, 1, 3, 2)


def _bidir(scan, init, ctx_out):
    yc = yl = None
    for reverse in (False, True):
        oc, state = scan("c", init, yc, reverse, ctx_out)
        if ctx_out:
            yc = oc
        yl, _ = scan("l", state, yl, reverse, True)
    return yc, yl


def _even_mixer(xc, xl, mods_c, mods_l, norm_g, p):
    (sh1c, sc1c, g1c), (sh1l, sc1l, g1l) = mods_c, mods_l
    w_in = jnp.concatenate([p["w_in"], p["a1"][0], p["a1"][1],
                            jnp.zeros((D_MODEL, EVEN_N - EVEN_IN - 2 * GLA_RANK), F32)], axis=1).astype(BF16)
    projs = {"c": _norm_proj(xc, norm_g, sc1c, sh1c, w_in, 256),
             "l": _norm_proj(xl, norm_g, sc1l, sh1l, w_in, 256)}
    mq_off = GLA_QK * 2 + GLA_V * 2
    qks = {s: _dwconv_silu(projs[s], mq_off, p["conv_w"], p["conv_b"]) for s in projs}
    gates_r = {s: _rows_to_chunks(projs[s][:, :, EVEN_IN - 16:EVEN_IN], ML_CHUNK) for s in projs}
    bsz = xl.shape[0]

    a2p = [jnp.zeros((LANES, GLA_QK), F32).at[16 + 16 * d:32 + 16 * d].set(p["a2"][d]).astype(BF16)
           for d in range(2)]
    ab = [p["ab"][d].reshape(1, GLA_QK) for d in range(2)]

    def gla(seg, state, prev, reverse, with_out):
        d = 1 if reverse else 0
        return _gla_scan(projs[seg], a2p[d], ab[d], state, prev, reverse)

    gla_init = jnp.zeros((bsz, 2, 2 * GLA_DV, LANES), F32)
    oc_gla, ol_gla = _bidir(gla, gla_init, True)

    def mlstm(seg, state, prev, reverse, with_out):
        return _mlstm_scan(qks[seg], projs[seg], gates_r[seg], p["gate_b"], state, prev, reverse)

    ml_init = (jnp.zeros((bsz, ML_HEADS, ML_D, ML_D), F32), jnp.zeros((bsz, ML_HEADS, 1, ML_D), F32),
               jnp.zeros((bsz, ML_HEADS, 1, LANES), F32))
    hc_ml, hl_ml = _bidir(mlstm, ml_init, True)

    w_out = p["w_out"].astype(BF16)
    xc = _even_out(oc_gla, hc_ml, projs["c"], p["gla_norm_g"], p["ml_norm_g"], w_out, xc, g1c, 256)
    xl = _even_out(ol_gla, hl_ml, projs["l"], p["gla_norm_g"], p["ml_norm_g"], w_out, xl, g1l, 512)
    return xc, xl


def _odd_mixer_last(xc, xl, mods_c, mods_l, norm_g, p):
    (sh1c, sc1c, _), (sh1l, sc1l, g1l) = mods_c, mods_l
    w_in = jnp.concatenate([p["w_in"], jnp.zeros((D_MODEL, ODD_N - ODD_IN), F32)], axis=1).astype(BF16)
    projs = {"c": _norm_proj(xc, norm_g, sc1c, sh1c, w_in, 256),
             "l": _norm_proj(xl, norm_g, sc1l, sh1l, w_in, 256)}
    xbcs = {s: _dwconv_silu(projs[s], D_INNER, p["conv_w"], p["conv_b"]) for s in projs}
    dt_rs = {s: _rows_to_chunks(projs[s][:, :, ODD_IN - 2 * SSD_HEADS:ODD_IN], SSD_CHUNK) for s in projs}
    bsz = xl.shape[0]

    def ssd(seg, state, prev, reverse, with_out):
        return _ssd_scan(xbcs[seg], projs[seg], dt_rs[seg], p["dt_bias"], p["a_log"], state, prev,
                         reverse, with_out)

    init = jnp.zeros((bsz, SSD_GROUPS, SSD_STATE, SSD_HPG * SSD_P), F32)
    _, yl = _bidir(ssd, init, False)
    return _odd_out(yl, xbcs["l"], projs["l"], p["d_skip"], p["norm_g"], p["w_out"].astype(BF16), xl, g1l, 256)


def kernel(x, c, ctx, c_ctx, mod_w, mod_b, norm_mix_g, norm_ffn_g, final_norm_g, ffn_w_up, ffn_conv_w, ffn_conv_b, ffn_w_down, even_w_in, gla_a1, gla_a2, gla_ab, ml_conv_w, ml_conv_b, ml_gate_b, gla_norm_g, ml_norm_g, even_w_out, ssd_w_in, ssd_conv_w, ssd_conv_b, ssd_dt_bias, ssd_a_log, ssd_d, ssd_norm_g, ssd_w_out):
    depth = mod_w.shape[0]
    assert depth == 2 and x.shape[1] % (GRID_W * 8) == 0
    bsz = x.shape[0]
    rows = -(-(bsz + 1) // 8) * 8
    cc = jnp.zeros((rows, D_MODEL), F32).at[:bsz].set(c).at[bsz].set(c_ctx)
    mod = _modulation(cc, mod_w, mod_b)

    def mods(layer, lo, hi):
        return [mod[layer, lo:hi, i * D_MODEL:(i + 1) * D_MODEL][:, None, :] for i in range(6)]

    xl, xc = x, ctx
    for layer in range(depth):
        last = layer == depth - 1
        sh1l, sc1l, g1l, sh2l, sc2l, g2l = mods(layer, 0, bsz)
        sh1c, sc1c, g1c, sh2c, sc2c, g2c = mods(layer, bsz, bsz + 1)
        if layer == 0:
            p = dict(w_in=even_w_in[0], a1=gla_a1[0], a2=gla_a2[0], ab=gla_ab[0], conv_w=ml_conv_w[0],
                     conv_b=ml_conv_b[0], gate_b=ml_gate_b[0], gla_norm_g=gla_norm_g[0],
                     ml_norm_g=ml_norm_g[0], w_out=even_w_out[0])
            xc, xl = _even_mixer(xc, xl, (sh1c, sc1c, g1c), (sh1l, sc1l, g1l), norm_mix_g[layer], p)
        else:
            p = dict(w_in=ssd_w_in[0], conv_w=ssd_conv_w[0], conv_b=ssd_conv_b[0], dt_bias=ssd_dt_bias[0],
                     a_log=ssd_a_log[0], d_skip=ssd_d[0], norm_g=ssd_norm_g[0], w_out=ssd_w_out[0])
            xl = _odd_mixer_last(xc, xl, (sh1c, sc1c, g1c), (sh1l, sc1l, g1l), norm_mix_g[layer], p)
        w_up = ffn_w_up[layer].astype(BF16)
        w_down = ffn_w_down[layer].astype(BF16)
        cw9 = ffn_conv_w[layer].reshape(9, 2 * D_FF)
        act = _ffn_up(xl, norm_ffn_g[layer], sc2l, sh2l, w_up, cw9, ffn_conv_b[layer], True)
        xl = _ffn_down(act, w_down, xl, g2l, final_norm_g, last, 512)
        if not last:
            act = _ffn_up(xc, norm_ffn_g[layer], sc2c, sh2c, w_up, cw9, ffn_conv_b[layer], False)
            xc = _ffn_down(act, w_down, xc, g2c, final_norm_g, False, 256)
    return xl
```

```python
import functools

import numpy as np
import jax
import jax.numpy as jnp
from jax import lax
from jax.experimental import pallas as pl
from jax.experimental.pallas import tpu as pltpu

F32 = jnp.float32
BF16 = jnp.bfloat16

D_MODEL = 1024
GRID_W = 64
EPS = 1e-6

GLA_HEADS, GLA_DK, GLA_DV, GLA_RANK, GLA_TAU, GLA_CHUNK = 4, 64, 128, 16, 16.0, 64
ML_HEADS, ML_D, ML_CHUNK = 4, 128, 64
D_INNER, SSD_HEADS, SSD_GROUPS, SSD_HPG, SSD_P, SSD_STATE, SSD_CHUNK = 2048, 32, 4, 8, 64, 128, 128
D_FF = 2816
GLA_QK, GLA_V, ML_W = GLA_HEADS * GLA_DK, GLA_HEADS * GLA_DV, ML_HEADS * ML_D
EVEN_IN = 3600
ODD_IN = 5184
SSD_BC = SSD_GROUPS * SSD_STATE
SSD_CONV_CH = D_INNER + 2 * SSD_BC

LANES = 128
SUBLANES = 8
EVEN_N = 3712
ODD_N = 5248
V7X_VMEM_BYTES = 64 * 1024 * 1024
VMEM_LIMIT = V7X_VMEM_BYTES - 8 * 1024 * 1024

GLA_NB = 2
ML_NB = 2
SSD_NB = 2

NEG_INF = float("-inf")


def _params(*sem):
    return pltpu.CompilerParams(dimension_semantics=sem, vmem_limit_bytes=VMEM_LIMIT)


def _sigmoid(x):
    return 1.0 / (1.0 + jnp.exp(-x))


def _silu(x):
    return x * _sigmoid(x)


def _softplus(x):
    return jnp.maximum(x, 0.0) + jnp.log(1.0 + jnp.exp(-jnp.abs(x)))


def _log_sigmoid(x):
    return -_softplus(-x)


def _hi_lo(x):
    hi = x.astype(BF16)
    lo = (x - hi.astype(F32)).astype(BF16)
    return hi, lo


def _dot(a, b):
    return jnp.dot(a.astype(BF16), b.astype(BF16), preferred_element_type=F32)


def _dot_nt(a, b):
    return lax.dot_general(a.astype(BF16), b.astype(BF16), (((1,), (1,)), ((), ())),
                           preferred_element_type=F32)


def _dot_tn(a, b):
    return lax.dot_general(a.astype(BF16), b.astype(BF16), (((0,), (0,)), ((), ())),
                           preferred_element_type=F32)


def _norm_mod(x, g, sc, sh):
    y = x * lax.rsqrt(jnp.mean(x * x, axis=-1, keepdims=True) + EPS)
    return (y * g) * (1.0 + sc) + sh


def _mod_kernel(c_ref, w_ref, b_ref, o_ref):
    s = _silu(c_ref[...])
    hi, lo = _hi_lo(s)
    w = w_ref[0]
    whi, wlo = _hi_lo(w)
    acc = jnp.dot(hi, whi, preferred_element_type=F32)
    acc += jnp.dot(lo, whi, preferred_element_type=F32)
    acc += jnp.dot(hi, wlo, preferred_element_type=F32)
    o_ref[0] = acc + b_ref[0]


def _modulation(cc, mod_w, mod_b):
    depth, d, n = mod_w.shape
    rows = cc.shape[0]
    tn = 1536
    return pl.pallas_call(
        _mod_kernel,
        grid=(depth, n // tn),
        in_specs=[pl.BlockSpec((rows, d), lambda l, j: (0, 0)),
                  pl.BlockSpec((1, d, tn), lambda l, j: (l, 0, j)),
                  pl.BlockSpec((1, 1, tn), lambda l, j: (l, 0, j))],
        out_specs=pl.BlockSpec((1, rows, tn), lambda l, j: (l, 0, j)),
        out_shape=jax.ShapeDtypeStruct((depth, rows, n), F32),
        compiler_params=_params("parallel", "parallel"),
        name="modulation",
    )(cc, mod_w, mod_b.reshape(depth, 1, n))


def _norm_proj_kernel(x_ref, g_ref, sc_ref, sh_ref, w_ref, o_ref):
    hb = _norm_mod(x_ref[0], g_ref[...], sc_ref[0], sh_ref[0]).astype(BF16)
    n = w_ref.shape[1]
    for s in range(0, n, 1024):
        e = min(s + 1024, n)
        o_ref[0, :, s:e] = jnp.dot(hb, w_ref[:, s:e], preferred_element_type=F32)


def _bvec_spec(v):
    d = v.shape[-1]
    if v.shape[0] == 1:
        return pl.BlockSpec((1, 1, d), lambda b, t: (0, 0, 0))
    return pl.BlockSpec((1, 1, d), lambda b, t: (b, 0, 0))


def _norm_proj(x, g, sc, sh, w, tm):
    bsz, t, d = x.shape
    n = w.shape[1]
    return pl.pallas_call(
        _norm_proj_kernel,
        grid=(bsz, t // tm),
        in_specs=[pl.BlockSpec((1, tm, d), lambda b, i: (b, i, 0)),
                  pl.BlockSpec((1, d), lambda b, i: (0, 0)),
                  _bvec_spec(sc), _bvec_spec(sh),
                  pl.BlockSpec((d, n), lambda b, i: (0, 0))],
        out_specs=pl.BlockSpec((1, tm, n), lambda b, i: (b, i, 0)),
        out_shape=jax.ShapeDtypeStruct((bsz, t, n), F32),
        compiler_params=_params("parallel", "parallel"),
        name="norm_proj",
    )(x, g.reshape(1, d), sc, sh, w)


def _dwconv_silu_kernel(x_ref, w_ref, b_ref, o_ref):
    x = x_ref[0]
    t = x.shape[0]
    tok = lax.broadcasted_iota(jnp.int32, x.shape, 0)
    prev = jnp.where(tok == 0, 0.0, pltpu.roll(x, 1, 0))
    nxt = jnp.where(tok == t - 1, 0.0, pltpu.roll(x, t - 1, 0))
    y = prev * w_ref[0:1] + x * w_ref[1:2] + nxt * w_ref[2:3] + b_ref[...]
    o_ref[0] = _silu(y)


def _dwconv_silu(x, col0, w, b):
    bsz, t, _ = x.shape
    ch = w.shape[1]
    ct = 256
    off = col0 // ct
    return pl.pallas_call(
        _dwconv_silu_kernel,
        grid=(bsz, ch // ct),
        in_specs=[pl.BlockSpec((1, t, ct), lambda b_, j: (b_, 0, off + j)),
                  pl.BlockSpec((3, ct), lambda b_, j: (0, j)),
                  pl.BlockSpec((1, ct), lambda b_, j: (0, j))],
        out_specs=pl.BlockSpec((1, t, ct), lambda b_, j: (b_, 0, j)),
        out_shape=jax.ShapeDtypeStruct((bsz, t, ch), F32),
        compiler_params=_params("parallel", "parallel"),
        name="dwconv_silu",
    )(x, w, b.reshape(1, ch))


def _flip2(a):
    return a[..., ::-1, ::-1].copy()


GLA_LEVELS = (32, 16, 8, 4, 2, 1)
GLA_U_ROWS = 14 * GLA_CHUNK + SUBLANES


def _gla_constants():
    L = GLA_CHUNK
    t = np.arange(L)[:, None]
    j = np.arange(L)[None, :]
    blocks = [(j <= t), (j > t)]
    uq, uk, masks = [], [], [np.eye(L)]
    for m in GLA_LEVELS:
        pos, blk = t % (2 * m), t // (2 * m)
        ref = blk * 2 * m + m - 1
        uq.append((pos >= m) & (j > ref) & (j <= t))
        uk.append((pos < m) & (j > t) & (j <= ref))
        masks.append((blk == blk.T) & (pos >= m) & (pos.T < m))
    blocks = [np.asarray(b_, np.float32) for b_ in blocks + uq + uk]
    masks = [np.asarray(m_, np.float32) for m_ in masks]
    us, mask2s = [], []
    for reverse in (False, True):
        bl = [_flip2(b_) for b_ in blocks] if reverse else blocks
        ms = [_flip2(m_) for m_ in masks] if reverse else masks
        us.append(np.concatenate(bl + [np.ones((SUBLANES, L), np.float32)], axis=0))
        mask2s.append(np.stack([np.concatenate([m_, m_], axis=0) for m_ in ms]))
    return jnp.asarray(np.stack(us), BF16), jnp.asarray(np.stack(mask2s), F32)


def _tri_pair(L):
    t = np.arange(L)[:, None]
    j = np.arange(L)[None, :]
    tri = np.stack([np.asarray(j <= t, np.float32), np.asarray(j >= t, np.float32)])
    return jnp.asarray(tri, BF16), jnp.asarray(tri.transpose(0, 2, 1), BF16)


def _const_spec(a):
    return pl.BlockSpec(a.shape, lambda b, c: (0,) * a.ndim)


def _state_spec(a, nb):
    return pl.BlockSpec((nb,) + a.shape[1:], lambda b, c: (b,) + (0,) * (a.ndim - 1))


def _seq_specs(nb, L, nc):
    fwd = lambda w_, i: pl.BlockSpec((nb, L, w_), lambda b, c: (b, c, i))
    bwd = lambda w_, i: pl.BlockSpec((nb, L, w_), lambda b, c: (b, nc - 1 - c, i))
    return fwd, bwd


def _chunk_specs(nb, rows, L, nc):
    fwd = pl.BlockSpec((nb, 1, rows, L), lambda b, c: (b, c, 0, 0))
    bwd = pl.BlockSpec((nb, 1, rows, L), lambda b, c: (b, nc - 1 - c, 0, 0))
    return fwd, bwd


def _round_robin(chains):
    chains = list(chains)
    while chains:
        alive = []
        for ch in chains:
            try:
                next(ch)
                alive.append(ch)
            except StopIteration:
                pass
        chains = alive


def _gla_tile(q, k, vj, e, mask_ref, d, st_ref, o_ref, first, same_head):
    L = GLA_CHUNK
    cum, e_end, c_end = e[0:L], e[L:2 * L], e[14 * L:14 * L + 1]
    q = q * (GLA_DK ** -0.5)
    vj_t = vj.T
    q_in = q * jnp.exp(cum)
    k_end = k * jnp.exp(e_end)
    st = st_ref[...]
    o_inter = _dot_nt(q_in, st)
    att = jnp.zeros((2 * L, L), F32)
    for lvl in range(7):
        ql = q if lvl == 0 else q * jnp.exp(e[(1 + lvl) * L:(2 + lvl) * L])
        kl = k if lvl == 0 else k * jnp.exp(e[(7 + lvl) * L:(8 + lvl) * L])
        lhs = jnp.concatenate([jnp.where(first, ql, 0.0), jnp.where(first, 0.0, ql)], axis=0)
        att = att + _dot_nt(lhs, kl) * mask_ref[d, lvl]
        yield
    o_a = _dot(att[:L], vj[:, :GLA_DV])
    o_b = _dot(att[L:], vj[:, GLA_DV:])
    upd = _dot(vj_t, k_end)
    yield
    o_ref[...] = jnp.concatenate([o_a, o_b], axis=1) + o_inter
    st_ref[...] = st * jnp.exp(c_end) + jnp.where(same_head, upd, 0.0)
    yield


def _gla_kernel(qf, kf, vf, lrf, qb, kb, vb, lrb, a2_ref, ab_ref, u_ref, mask_ref, s0_ref,
                of_ref, ob_ref, s_out_ref, s_scr, *, n_chunks, nb):
    c = pl.program_id(1)

    @pl.when(c == 0)
    def _():
        s_scr[...] = s0_ref[...]

    first = lax.broadcasted_iota(jnp.int32, (GLA_CHUNK, LANES), 1) < GLA_DK
    row = lax.broadcasted_iota(jnp.int32, (2 * GLA_DV, LANES), 0)
    lane = lax.broadcasted_iota(jnp.int32, (2 * GLA_DV, LANES), 1)
    same_head = (row < GLA_DV) == (lane < GLA_DK)
    sides = ((qf, kf, vf, lrf, of_ref), (qb, kb, vb, lrb, ob_ref))
    samples = [(d, bi) for d in range(2) for bi in range(nb)]
    las = [_log_sigmoid(_dot(sides[d][3][bi], a2_ref[d]) + ab_ref[d]) * (1.0 / GLA_TAU)
           for d, bi in samples]
    es = []
    for (d, bi), la in zip(samples, las):
        hi, lo = _hi_lo(la)
        e2 = jnp.dot(u_ref[d], jnp.concatenate([hi, lo], axis=1), preferred_element_type=F32)
        es.append(e2[:, :GLA_QK] + e2[:, GLA_QK:])
    chains = []
    for (d, bi), e in zip(samples, es):
        q_ref, k_ref, v_ref, _, o_ref = sides[d]
        for j in range(2):
            sl = slice(j * LANES, (j + 1) * LANES)
            vsl = slice(2 * j * GLA_DV, (2 * j + 2) * GLA_DV)
            chains.append(_gla_tile(q_ref[bi, :, sl], k_ref[bi, :, sl], v_ref[bi, :, vsl], e[:, sl], mask_ref,
                                    d, s_scr.at[bi, d, j], o_ref.at[bi, :, vsl], first, same_head))
    _round_robin(chains)

    @pl.when(c == n_chunks - 1)
    def _():
        s_out_ref[...] = s_scr[...]


def _gla_scan(proj, a2p, ab, s0):
    bsz, t, _ = proj.shape
    L, nb = GLA_CHUNK, GLA_NB
    nc = t // L
    u, mask2 = _gla_constants()
    fwd, bwd = _seq_specs(nb, L, nc)
    seqs = lambda s: [s(GLA_QK, 0), s(GLA_QK, 1), s(GLA_V, 1), s(LANES, 28)]
    return pl.pallas_call(
        functools.partial(_gla_kernel, n_chunks=nc, nb=nb),
        grid=(bsz // nb, nc),
        in_specs=seqs(fwd) + seqs(bwd) + [_const_spec(a2p), _const_spec(ab), _const_spec(u),
                                          _const_spec(mask2), _state_spec(s0, nb)],
        out_specs=[fwd(GLA_V, 0), bwd(GLA_V, 0), _state_spec(s0, nb)],
        out_shape=[jax.ShapeDtypeStruct((bsz, t, GLA_V), F32)] * 2 + [jax.ShapeDtypeStruct(s0.shape, F32)],
        scratch_shapes=[pltpu.VMEM((nb,) + s0.shape[1:], F32)],
        compiler_params=_params("parallel", "arbitrary"),
        name="gla_scan",
    )(*([proj] * 8), a2p, ab, u, mask2, s0)


def _mlstm_gates(gc, gr, gbc, gbr, tri_b, trit_b):
    gates_c = gc + gbc
    gates_r = gr + gbr
    hi, lo = _hi_lo(_log_sigmoid(gates_c))
    b2 = jnp.dot(tri_b, jnp.concatenate([hi, lo], axis=1), preferred_element_type=F32)
    b_cols = b2[:, :LANES] + b2[:, LANES:]
    hi, lo = _hi_lo(_log_sigmoid(gates_r))
    b2 = jnp.dot(jnp.concatenate([hi, lo], axis=0), trit_b, preferred_element_type=F32)
    b_rows = b2[:16] + b2[16:]
    return gates_c, gates_r, b_cols, b_rows


def _mlstm_head(q, k, v, gates, tri, d, h, c_ref, n_ref, m_ref, o_ref):
    gates_c, gates_r, b_cols, b_rows = gates
    end = 0 if d else ML_CHUNK - 1
    ci_, cf_ = 8 * d + h, 8 * d + 4 + h
    b_c, b_r = b_cols[:, cf_:cf_ + 1], b_rows[cf_:cf_ + 1, :]
    li_c, li_r = gates_c[:, ci_:ci_ + 1], gates_r[ci_:ci_ + 1, :]
    b_end = b_r[:, end:end + 1]
    m = m_ref[h][:, 0:1]
    cmat = c_ref[h]
    nvec = n_ref[h]
    k = k * (ML_D ** -0.5)
    qk = _dot_nt(q, k)
    q_c = _dot(q, cmat)
    a_in = b_c + m
    d_log = jnp.where(tri, b_c - b_r + li_r, NEG_INF)
    m_t = jnp.maximum(a_in, jnp.max(d_log, axis=1, keepdims=True))
    aw = jnp.exp(a_in - m_t)
    w_end = b_end - b_r + li_r
    m_new = jnp.maximum(b_end + m, jnp.max(w_end, axis=1, keepdims=True))
    decay = jnp.exp(b_end + m - m_new)
    kw = k * jnp.exp(b_end - b_c + li_c - m_new)
    kw_t = kw.T
    yield
    sc = qk * jnp.exp(d_log - m_t)
    s_v = _dot(sc, v)
    yield
    num = aw * q_c + s_v
    den = aw * jnp.sum(q * nvec, axis=1, keepdims=True) + jnp.sum(sc, axis=1, keepdims=True)
    o_ref[...] = num / jnp.maximum(jnp.abs(den), jnp.exp(-m_t))
    c_ref[h] = decay * cmat + _dot(kw_t, v)
    n_ref[h] = decay * nvec + jnp.sum(kw, axis=0, keepdims=True)
    m_ref[h] = jnp.broadcast_to(m_new, (1, LANES))
    yield


def _mlstm_kernel(qf, kf, vf, gcf, grf, qb, kb, vb, gcb, grb, gbc_ref, gbr_ref, tri_ref, trit_ref,
                  c0_ref, n0_ref, m0_ref, of_ref, ob_ref, c_out_ref, n_out_ref, m_out_ref,
                  c_scr, n_scr, m_scr, *, n_chunks, nb):
    c = pl.program_id(1)

    @pl.when(c == 0)
    def _():
        c_scr[...] = c0_ref[...]
        n_scr[...] = n0_ref[...]
        m_scr[...] = m0_ref[...]

    chains = []
    for d, (q_ref, k_ref, v_ref, gc_ref, gr_ref, o_ref) in enumerate(((qf, kf, vf, gcf, grf, of_ref),
                                                                      (qb, kb, vb, gcb, grb, ob_ref))):
        tri = tri_ref[d] > 0.5
        for bi in range(nb):
            gates = _mlstm_gates(gc_ref[bi], gr_ref[bi, 0], gbc_ref[...], gbr_ref[...], tri_ref[d],
                                 trit_ref[d])
            for h in range(ML_HEADS):
                sl = slice(h * ML_D, (h + 1) * ML_D)
                chains.append(_mlstm_head(q_ref[bi, :, sl], k_ref[bi, :, sl], v_ref[bi, :, sl], gates, tri,
                                          d, h, c_scr.at[bi, d], n_scr.at[bi, d], m_scr.at[bi, d],
                                          o_ref.at[bi, :, sl]))
    _round_robin(chains)

    @pl.when(c == n_chunks - 1)
    def _():
        c_out_ref[...] = c_scr[...]
        n_out_ref[...] = n_scr[...]
        m_out_ref[...] = m_scr[...]


def _mlstm_scan(qk, proj, gates_r, gate_b, state):
    bsz, t, _ = proj.shape
    L, nb = ML_CHUNK, ML_NB
    nc = t // L
    tri_b, trit_b = _tri_pair(L)
    gbc = jnp.zeros((1, LANES), F32).at[0, :16].set(gate_b)
    gbr = gate_b.reshape(16, 1)
    fwd, bwd = _seq_specs(nb, L, nc)
    gfwd, gbwd = _chunk_specs(nb, 16, L, nc)
    seqs = lambda s, g: [s(ML_W, 0), s(ML_W, 1), s(ML_W, 5), s(LANES, 28), g]
    st_specs = [_state_spec(a, nb) for a in state]
    outs = pl.pallas_call(
        functools.partial(_mlstm_kernel, n_chunks=nc, nb=nb),
        grid=(bsz // nb, nc),
        in_specs=seqs(fwd, gfwd) + seqs(bwd, gbwd) + [_const_spec(gbc), _const_spec(gbr),
                                                      _const_spec(tri_b), _const_spec(trit_b)] + st_specs,
        out_specs=[fwd(ML_W, 0), bwd(ML_W, 0)] + st_specs,
        out_shape=[jax.ShapeDtypeStruct((bsz, t, ML_W), F32)] * 2
                  + [jax.ShapeDtypeStruct(a.shape, F32) for a in state],
        scratch_shapes=[pltpu.VMEM((nb,) + a.shape[1:], F32) for a in state],
        compiler_params=_params("parallel", "arbitrary"),
        name="mlstm_scan",
    )(*([qk, qk, proj, proj, gates_r] * 2), gbc, gbr, tri_b, trit_b, *state)
    return outs[0], outs[1], tuple(outs[2:])


def _ssd_steps(dtc, dtr, bias_c, bias_r, a_c, a_r, tri_b, trit_b):
    dt_c = _softplus(dtc + bias_c)
    dt_r = _softplus(dtr + bias_r)
    hi, lo = _hi_lo(dt_c * -jnp.exp(a_c))
    c2 = jnp.dot(tri_b, jnp.concatenate([hi, lo], axis=1), preferred_element_type=F32)
    cum_c = c2[:, :LANES] + c2[:, LANES:]
    hi, lo = _hi_lo(dt_r * -jnp.exp(a_r))
    c2 = jnp.dot(jnp.concatenate([hi, lo], axis=0), trit_b, preferred_element_type=F32)
    cum_r = c2[:SSD_HEADS * 2] + c2[SSD_HEADS * 2:]
    return dt_c, dt_r, cum_c, cum_r


def _ssd_group(x_ref, bg, cg, steps, tri, d, g, s_ref, y_ref, first):
    dt_c, dt_r, cum_c, cum_r = steps
    end = 0 if d else SSD_CHUNK - 1
    sg = s_ref[...]
    bg_t = bg.T
    if y_ref is not None:
        cb = _dot_nt(cg, bg)
        y_inter = _dot(cg, sg)
        yield
    xw_tiles, dec_tiles = [], []
    for j in range(SSD_HPG // 2):
        ha = d * SSD_HEADS + g * SSD_HPG + 2 * j
        hb = ha + 1
        tile = slice(j * LANES, (j + 1) * LANES)
        xt = x_ref[:, tile]
        cum_a, cum_b = cum_c[:, ha:ha + 1], cum_c[:, hb:hb + 1]
        end_a, end_b = cum_a[end:end + 1], cum_b[end:end + 1]
        if y_ref is not None:
            ws = []
            for hh, cum_h in ((ha, cum_a), (hb, cum_b)):
                seg = jnp.exp(jnp.where(tri, cum_h - cum_r[hh:hh + 1, :], NEG_INF))
                ws.append(seg * cb * dt_r[hh:hh + 1, :])
            w2 = jnp.concatenate(ws, axis=1)
            x2 = jnp.concatenate([jnp.where(first, xt, 0.0), jnp.where(first, 0.0, xt)], axis=0)
            e_cum = jnp.where(first, jnp.exp(cum_a), jnp.exp(cum_b))
            y_ref[:, tile] = _dot(w2, x2) + e_cum * y_inter[:, tile]
            yield
        w_end = jnp.where(first, jnp.exp(end_a - cum_a) * dt_c[:, ha:ha + 1],
                          jnp.exp(end_b - cum_b) * dt_c[:, hb:hb + 1])
        xw_tiles.append(xt * w_end)
        dec_tiles.append(jnp.where(first[0:1], jnp.exp(end_a), jnp.exp(end_b)))
    xw = jnp.concatenate(xw_tiles, axis=1)
    dec = jnp.concatenate(dec_tiles, axis=1)
    s_ref[...] = dec * sg + _dot(bg_t, xw)
    yield


def _ssd_kernel(xf, bmf, cmf, dtcf, dtrf, xb, bmb, cmb, dtcb, dtrb, bias_c_ref, bias_r_ref,
                a_c_ref, a_r_ref, tri_ref, trit_ref, s0_ref, *rest, n_chunks, nb, with_out):
    yf_ref, yb_ref = (rest[0], rest[1]) if with_out else (None, None)
    s_out_ref, s_scr = rest[-2:]
    c = pl.program_id(1)

    @pl.when(c == 0)
    def _():
        s_scr[...] = s0_ref[...]

    first = lax.broadcasted_iota(jnp.int32, (SSD_CHUNK, LANES), 1) < SSD_P
    chains = []
    for d, (x_ref, bm_ref, cm_ref, dtc_ref, dtr_ref, y_ref) in enumerate(
            ((xf, bmf, cmf, dtcf, dtrf, yf_ref), (xb, bmb, cmb, dtcb, dtrb, yb_ref))):
        tri = tri_ref[d] > 0.5
        for bi in range(nb):
            steps = _ssd_steps(dtc_ref[bi], dtr_ref[bi, 0], bias_c_ref[...], bias_r_ref[...], a_c_ref[...],
                               a_r_ref[...], tri_ref[d], trit_ref[d])
            for g in range(SSD_GROUPS):
                nsl = slice(g * SSD_STATE, (g + 1) * SSD_STATE)
                gsl = slice(g * SSD_HPG * SSD_P, (g + 1) * SSD_HPG * SSD_P)
                chains.append(_ssd_group(x_ref.at[bi, :, gsl], bm_ref[bi, :, nsl], cm_ref[bi, :, nsl], steps,
                                         tri, d, g, s_scr.at[bi, d, g],
                                         None if y_ref is None else y_ref.at[bi, :, gsl], first))
    _round_robin(chains)

    @pl.when(c == n_chunks - 1)
    def _():
        s_out_ref[...] = s_scr[...]


def _ssd_scan(xbc, proj, dt_r, dt_bias, a_log, s0, with_out):
    bsz, t, _ = xbc.shape
    L, nb = SSD_CHUNK, SSD_NB
    nc = t // L
    tri_b, trit_b = _tri_pair(L)
    nh = 2 * SSD_HEADS
    bias_c = jnp.zeros((1, LANES), F32).at[0, :nh].set(dt_bias.reshape(nh))
    a_c = jnp.zeros((1, LANES), F32).at[0, :nh].set(a_log.reshape(nh))
    bias_r, a_r = dt_bias.reshape(nh, 1), a_log.reshape(nh, 1)
    fwd, bwd = _seq_specs(nb, L, nc)
    gfwd, gbwd = _chunk_specs(nb, nh, L, nc)
    nbc = D_INNER // SSD_BC
    seqs = lambda s, g: [s(D_INNER, 0), s(SSD_BC, nbc), s(SSD_BC, nbc + 1), s(LANES, ODD_IN // LANES), g]
    consts = [bias_c, bias_r, a_c, a_r, tri_b, trit_b]
    out_specs, out_shape = [_state_spec(s0, nb)], [jax.ShapeDtypeStruct(s0.shape, F32)]
    if with_out:
        out_specs = [fwd(D_INNER, 0), bwd(D_INNER, 0)] + out_specs
        out_shape = [jax.ShapeDtypeStruct((bsz, t, D_INNER), F32)] * 2 + out_shape
    outs = pl.pallas_call(
        functools.partial(_ssd_kernel, n_chunks=nc, nb=nb, with_out=with_out),
        grid=(bsz // nb, nc),
        in_specs=seqs(fwd, gfwd) + seqs(bwd, gbwd) + [_const_spec(a) for a in consts] + [_state_spec(s0, nb)],
        out_specs=out_specs, out_shape=out_shape,
        scratch_shapes=[pltpu.VMEM((nb,) + s0.shape[1:], F32)],
        compiler_params=_params("parallel", "arbitrary"),
        name="ssd_scan",
    )(*([xbc, xbc, xbc, proj, dt_r] * 2), *consts, s0)
    return (outs[0], outs[1], outs[2]) if with_out else (None, None, outs[0])


def _head_rms(y, g, width):
    parts = []
    for s in range(0, y.shape[1], width):
        p = y[:, s:s + width]
        parts.append(p * lax.rsqrt(jnp.mean(p * p, axis=-1, keepdims=True) + EPS))
    return jnp.concatenate(parts, axis=1) * g


def _even_out_kernel(ogf_ref, ogb_ref, hmf_ref, hmb_ref, gg_ref, mo_ref, gn_ref, mn_ref, w_ref,
                     x_ref, gate_ref, o_ref):
    gla = _head_rms(ogf_ref[0] + ogb_ref[0], gn_ref[...], GLA_DV) * _silu(gg_ref[0])
    ml = _head_rms(_sigmoid(mo_ref[0]) * (hmf_ref[0] + hmb_ref[0]), mn_ref[...], ML_D)
    y = _dot(gla, w_ref[:GLA_V]) + _dot(ml, w_ref[GLA_V:])
    o_ref[0] = x_ref[0] + gate_ref[0] * y


def _even_out(o_gla, h_ml, proj, gla_norm_g, ml_norm_g, w_out, x, gate, tm):
    bsz, t, d = x.shape
    tok = lambda w_, i: pl.BlockSpec((1, tm, w_), lambda b, s: (b, s, i))
    const = lambda a: pl.BlockSpec(a.shape, lambda b, s: (0,) * a.ndim)
    gn, mn = gla_norm_g.reshape(1, GLA_V), ml_norm_g.reshape(1, ML_W)
    return pl.pallas_call(
        _even_out_kernel,
        grid=(bsz, t // tm),
        in_specs=[tok(GLA_V, 0)] * 2 + [tok(ML_W, 0)] * 2 + [tok(GLA_V, 2), tok(ML_W, 6), const(gn),
                                                             const(mn), const(w_out), tok(d, 0),
                                                             _bvec_spec(gate)],
        out_specs=tok(d, 0),
        out_shape=jax.ShapeDtypeStruct(x.shape, F32),
        compiler_params=_params("parallel", "parallel"),
        name="even_out",
    )(*o_gla, *h_ml, proj, proj, gn, mn, w_out, x, gate)


def _odd_out_kernel(yf_ref, yb_ref, xs_ref, z_ref, dsk_ref, ng_ref, w_ref, x_ref, gate_ref, o_ref):
    y = (yf_ref[0] + yb_ref[0] + dsk_ref[...] * xs_ref[0]) * _silu(z_ref[0])
    y = _head_rms(y, ng_ref[...], D_INNER // SSD_GROUPS)
    o_ref[0] = x_ref[0] + gate_ref[0] * _dot(y, w_ref[...])


def _odd_out(y, xbc, proj, d_skip, norm_g, w_out, x, gate, tm):
    bsz, t, d = x.shape
    tok = lambda w_, i: pl.BlockSpec((1, tm, w_), lambda b, s: (b, s, i))
    const = lambda a: pl.BlockSpec(a.shape, lambda b, s: (0,) * a.ndim)
    dsk = jnp.repeat(d_skip, SSD_P).reshape(1, D_INNER)
    ng = norm_g.reshape(1, D_INNER)
    return pl.pallas_call(
        _odd_out_kernel,
        grid=(bsz, t // tm),
        in_specs=[tok(D_INNER, 0)] * 4 + [const(dsk), const(ng), const(w_out), tok(d, 0), _bvec_spec(gate)],
        out_specs=tok(d, 0),
        out_shape=jax.ShapeDtypeStruct(x.shape, F32),
        compiler_params=_params("parallel", "parallel"),
        name="odd_out",
    )(*y, xbc, proj, dsk, ng, w_out, x, gate)


FFN_TB = 256
FFN_PAD = GRID_W + SUBLANES
FFN_CT = 256


def _ffn_conv_block(u_scr, ul_scr, ur_scr, r0, w9, b, masks):
    drs = (-1, 0, 1) if masks is not None else (0,)
    acc_l = acc_c = acc_r = None
    for dr in drs:
        rows = pl.ds(FFN_PAD + r0 + GRID_W * dr, FFN_TB)
        i = 3 * (dr + 1)
        tl = ul_scr[rows, :] * w9[i:i + 1]
        tc = u_scr[rows, :] * w9[i + 1:i + 2]
        tr = ur_scr[rows, :] * w9[i + 2:i + 3]
        acc_l = tl if acc_l is None else acc_l + tl
        acc_c = tc if acc_c is None else acc_c + tc
        acc_r = tr if acc_r is None else acc_r + tr
    if masks is not None:
        acc_l = jnp.where(masks[0], 0.0, acc_l)
        acc_r = jnp.where(masks[1], 0.0, acc_r)
    return acc_l + acc_c + acc_r + b


def _ffn_up_kernel(x_ref, g_ref, sc_ref, sh_ref, wa_ref, wg_ref, cwa_ref, cwg_ref, cba_ref, cbg_ref,
                   o_ref, h_scr, *u_scrs, grid_conv):
    t = x_ref.shape[1]
    ct = o_ref.shape[2]

    @pl.when(pl.program_id(1) == 0)
    def _():
        h_scr[...] = _norm_mod(x_ref[0], g_ref[...], sc_ref[0], sh_ref[0]).astype(BF16)
        zeros = jnp.zeros((FFN_PAD, ct), F32)
        for scr in u_scrs:
            scr[pl.ds(0, FFN_PAD), :] = zeros
            scr[pl.ds(FFN_PAD + t, FFN_PAD), :] = zeros

    if grid_conv:
        col = lax.broadcasted_iota(jnp.int32, (FFN_TB, ct), 0) % GRID_W
        masks = (col == 0, col == GRID_W - 1)
    else:
        masks = None
    halves = ((wa_ref, cwa_ref, cba_ref, u_scrs[0:3]), (wg_ref, cwg_ref, cbg_ref, u_scrs[3:6]))

    def up_block(blk):
        r0 = FFN_PAD + blk * FFN_TB
        hb = h_scr[pl.ds(blk * FFN_TB, FFN_TB), :]
        for w_ref, _, _, (u_scr, ul_scr, ur_scr) in halves:
            u_scr[pl.ds(r0, FFN_TB), :] = jnp.dot(hb, w_ref[...], preferred_element_type=F32)
            ul_scr[pl.ds(r0, FFN_TB), :] = u_scr[pl.ds(r0 - 1, FFN_TB), :]
            ur_scr[pl.ds(r0 - SUBLANES, FFN_TB), :] = u_scr[pl.ds(r0 - SUBLANES + 1, FFN_TB), :]

    def conv_block(blk):
        r0 = blk * FFN_TB
        a, g = [_ffn_conv_block(*scrs, r0, cw_ref, cb_ref[...], masks) for _, cw_ref, cb_ref, scrs in halves]
        o_ref[0, pl.ds(r0, FFN_TB), :] = (_silu(g) * a).astype(BF16)

    nblk = t // FFN_TB
    for blk in range(nblk):
        up_block(blk)
        if blk >= 1:
            conv_block(blk - 1)
    for _, _, _, (u_scr, _, ur_scr) in halves:
        r0 = FFN_PAD + t - SUBLANES
        ur_scr[pl.ds(r0, SUBLANES), :] = u_scr[pl.ds(r0 + 1, SUBLANES), :]
    conv_block(nblk - 1)


def _ffn_up(x, g, sc, sh, w_up, conv_w9, conv_b, grid_conv):
    bsz, t, d = x.shape
    ct = FFN_CT
    nct = D_FF // ct
    cb = conv_b.reshape(1, 2 * D_FF)
    return pl.pallas_call(
        functools.partial(_ffn_up_kernel, grid_conv=grid_conv),
        grid=(bsz, nct),
        in_specs=[pl.BlockSpec((1, t, d), lambda b, j: (b, 0, 0)),
                  pl.BlockSpec((1, d), lambda b, j: (0, 0)),
                  _bvec_spec(sc), _bvec_spec(sh),
                  pl.BlockSpec((d, ct), lambda b, j: (0, j)),
                  pl.BlockSpec((d, ct), lambda b, j: (0, nct + j)),
                  pl.BlockSpec((9, ct), lambda b, j: (0, j)),
                  pl.BlockSpec((9, ct), lambda b, j: (0, nct + j)),
                  pl.BlockSpec((1, ct), lambda b, j: (0, j)),
                  pl.BlockSpec((1, ct), lambda b, j: (0, nct + j))],
        out_specs=pl.BlockSpec((1, t, ct), lambda b, j: (b, 0, j)),
        out_shape=jax.ShapeDtypeStruct((bsz, t, D_FF), BF16),
        scratch_shapes=[pltpu.VMEM((t, d), BF16)] + [pltpu.VMEM((t + 2 * FFN_PAD, ct), F32)] * 6,
        compiler_params=_params("parallel", "arbitrary"),
        name="ffn_up",
    )(x, g.reshape(1, d), sc, sh, w_up, w_up, conv_w9, conv_w9, cb, cb)


def _ffn_down_kernel(a_ref, w_ref, x_ref, gate_ref, fg_ref, o_ref, *, final_norm):
    y = x_ref[0] + gate_ref[0] * jnp.dot(a_ref[0], w_ref[...], preferred_element_type=F32)
    if final_norm:
        y = y * lax.rsqrt(jnp.mean(y * y, axis=-1, keepdims=True) + EPS) * fg_ref[...]
    o_ref[0] = y


def _ffn_down(act, w_down, x, gate, final_g, final_norm, tm):
    bsz, t, d = x.shape
    tok = lambda w_: pl.BlockSpec((1, tm, w_), lambda b, s: (b, s, 0))
    const = lambda a: pl.BlockSpec(a.shape, lambda b, s: (0,) * a.ndim)
    fg = final_g.reshape(1, d)
    return pl.pallas_call(
        functools.partial(_ffn_down_kernel, final_norm=final_norm),
        grid=(bsz, t // tm),
        in_specs=[tok(D_FF), const(w_down), tok(d), _bvec_spec(gate), const(fg)],
        out_specs=tok(d),
        out_shape=jax.ShapeDtypeStruct(x.shape, F32),
        compiler_params=_params("parallel", "parallel"),
        name="ffn_down",
    )(act, w_down, x, gate, fg)


def _rows_to_chunks(a, L):
    bsz, t, ch = a.shape
    return a.reshape(bsz, t // L, L, ch).transpose(0, 1, 3, 2)


def _even_mixer(xc, xl, mods_c, mods_l, norm_g, p):
    (sh1c, sc1c, g1c), (sh1l, sc1l, g1l) = mods_c, mods_l
    w_in = jnp.concatenate([p["w_in"], p["a1"][0], p["a1"][1],
                            jnp.zeros((D_MODEL, EVEN_N - EVEN_IN - 2 * GLA_RANK), F32)], axis=1).astype(BF16)
    projs = {"c": _norm_proj(xc, norm_g, sc1c, sh1c, w_in, 256),
             "l": _norm_proj(xl, norm_g, sc1l, sh1l, w_in, 256)}
    mq_off = GLA_QK * 2 + GLA_V * 2
    qks = {s: _dwconv_silu(projs[s], mq_off, p["conv_w"], p["conv_b"]) for s in projs}
    gates_r = {s: _rows_to_chunks(projs[s][:, :, EVEN_IN - 16:EVEN_IN], ML_CHUNK) for s in projs}
    bsz = xl.shape[0]

    a2p = jnp.stack([jnp.zeros((LANES, GLA_QK), F32).at[16 + 16 * d:32 + 16 * d].set(p["a2"][d])
                     for d in range(2)]).astype(BF16)
    ab = p["ab"].reshape(2, 1, GLA_QK)
    gla_state = jnp.zeros((bsz, 2, 2, 2 * GLA_DV, LANES), F32)
    ml_state = (jnp.zeros((bsz, 2, ML_HEADS, ML_D, ML_D), F32), jnp.zeros((bsz, 2, ML_HEADS, 1, ML_D), F32),
                jnp.zeros((bsz, 2, ML_HEADS, 1, LANES), F32))
    o_gla, h_ml = {}, {}
    for s in ("c", "l"):
        of, ob, gla_state = _gla_scan(projs[s], a2p, ab, gla_state)
        o_gla[s] = (of, ob)
        hf, hb, ml_state = _mlstm_scan(qks[s], projs[s], gates_r[s], p["gate_b"], ml_state)
        h_ml[s] = (hf, hb)

    w_out = p["w_out"].astype(BF16)
    xc = _even_out(o_gla["c"], h_ml["c"], projs["c"], p["gla_norm_g"], p["ml_norm_g"], w_out, xc, g1c, 256)
    xl = _even_out(o_gla["l"], h_ml["l"], projs["l"], p["gla_norm_g"], p["ml_norm_g"], w_out, xl, g1l, 512)
    return xc, xl


def _odd_mixer_last(xc, xl, mods_c, mods_l, norm_g, p):
    (sh1c, sc1c, _), (sh1l, sc1l, g1l) = mods_c, mods_l
    w_in = jnp.concatenate([p["w_in"], jnp.zeros((D_MODEL, ODD_N - ODD_IN), F32)], axis=1).astype(BF16)
    projs = {"c": _norm_proj(xc, norm_g, sc1c, sh1c, w_in, 256),
             "l": _norm_proj(xl, norm_g, sc1l, sh1l, w_in, 256)}
    xbcs = {s: _dwconv_silu(projs[s], D_INNER, p["conv_w"], p["conv_b"]) for s in projs}
    dt_rs = {s: _rows_to_chunks(projs[s][:, :, ODD_IN - 2 * SSD_HEADS:ODD_IN], SSD_CHUNK) for s in projs}
    bsz = xl.shape[0]
    state = jnp.zeros((bsz, 2, SSD_GROUPS, SSD_STATE, SSD_HPG * SSD_P), F32)
    _, _, state = _ssd_scan(xbcs["c"], projs["c"], dt_rs["c"], p["dt_bias"], p["a_log"], state, False)
    yf, yb, _ = _ssd_scan(xbcs["l"], projs["l"], dt_rs["l"], p["dt_bias"], p["a_log"], state, True)
    return _odd_out((yf, yb), xbcs["l"], projs["l"], p["d_skip"], p["norm_g"], p["w_out"].astype(BF16),
                    xl, g1l, 256)


def kernel(x, c, ctx, c_ctx, mod_w, mod_b, norm_mix_g, norm_ffn_g, final_norm_g, ffn_w_up, ffn_conv_w, ffn_conv_b, ffn_w_down, even_w_in, gla_a1, gla_a2, gla_ab, ml_conv_w, ml_conv_b, ml_gate_b, gla_norm_g, ml_norm_g, even_w_out, ssd_w_in, ssd_conv_w, ssd_conv_b, ssd_dt_bias, ssd_a_log, ssd_d, ssd_norm_g, ssd_w_out):
    depth = mod_w.shape[0]
    bsz = x.shape[0]
    assert depth == 2 and x.shape[1] % FFN_TB == 0 and ctx.shape[1] % FFN_TB == 0
    assert bsz % GLA_NB == 0 and bsz % ML_NB == 0 and bsz % SSD_NB == 0
    rows = -(-(bsz + 1) // SUBLANES) * SUBLANES
    cc = jnp.zeros((rows, D_MODEL), F32).at[:bsz].set(c).at[bsz].set(c_ctx)
    mod = _modulation(cc, mod_w, mod_b)

    def mods(layer, lo, hi):
        return [mod[layer, lo:hi, i * D_MODEL:(i + 1) * D_MODEL][:, None, :] for i in range(6)]

    xl, xc = x, ctx
    for layer in range(depth):
        last = layer == depth - 1
        sh1l, sc1l, g1l, sh2l, sc2l, g2l = mods(layer, 0, bsz)
        sh1c, sc1c, g1c, sh2c, sc2c, g2c = mods(layer, bsz, bsz + 1)
        if layer == 0:
            p = dict(w_in=even_w_in[0], a1=gla_a1[0], a2=gla_a2[0], ab=gla_ab[0], conv_w=ml_conv_w[0],
                     conv_b=ml_conv_b[0], gate_b=ml_gate_b[0], gla_norm_g=gla_norm_g[0],
                     ml_norm_g=ml_norm_g[0], w_out=even_w_out[0])
            xc, xl = _even_mixer(xc, xl, (sh1c, sc1c, g1c), (sh1l, sc1l, g1l), norm_mix_g[layer], p)
        else:
            p = dict(w_in=ssd_w_in[0], conv_w=ssd_conv_w[0], conv_b=ssd_conv_b[0], dt_bias=ssd_dt_bias[0],
                     a_log=ssd_a_log[0], d_skip=ssd_d[0], norm_g=ssd_norm_g[0], w_out=ssd_w_out[0])
            xl = _odd_mixer_last(xc, xl, (sh1c, sc1c, g1c), (sh1l, sc1l, g1l), norm_mix_g[layer], p)
        w_up = ffn_w_up[layer].astype(BF16)
        w_down = ffn_w_down[layer].astype(BF16)
        cw9 = ffn_conv_w[layer].reshape(9, 2 * D_FF)
        act = _ffn_up(xl, norm_ffn_g[layer], sc2l, sh2l, w_up, cw9, ffn_conv_b[layer], True)
        xl = _ffn_down(act, w_down, xl, g2l, final_norm_g, last, 512)
        if not last:
            act = _ffn_up(xc, norm_ffn_g[layer], sc2c, sh2c, w_up, cw9, ffn_conv_b[layer], False)
            xc = _ffn_down(act, w_down, xc, g2c, final_norm_g, False, 256)
    return xl
```

```python
import functools

import numpy as np
import jax
import jax.numpy as jnp
from jax import lax
from jax.experimental import pallas as pl
from jax.experimental.pallas import tpu as pltpu

F32 = jnp.float32
BF16 = jnp.bfloat16

D_MODEL = 1024
GRID_W = 64
EPS = 1e-6

GLA_HEADS, GLA_DK, GLA_DV, GLA_RANK, GLA_TAU, GLA_CHUNK = 4, 64, 128, 16, 16.0, 64
ML_HEADS, ML_D, ML_CHUNK = 4, 128, 64
D_INNER, SSD_HEADS, SSD_GROUPS, SSD_HPG, SSD_P, SSD_STATE, SSD_CHUNK = 2048, 32, 4, 8, 64, 128, 128
D_FF = 2816
GLA_QK, GLA_V, ML_W = GLA_HEADS * GLA_DK, GLA_HEADS * GLA_DV, ML_HEADS * ML_D
EVEN_IN = 3600
ODD_IN = 5184
SSD_BC = SSD_GROUPS * SSD_STATE
SSD_CONV_CH = D_INNER + 2 * SSD_BC

LANES = 128
SUBLANES = 8
EVEN_N = 3712
ODD_N = 5248
V7X_VMEM_BYTES = 64 * 1024 * 1024
VMEM_LIMIT = V7X_VMEM_BYTES - 8 * 1024 * 1024

GLA_NB = 4
ML_NB = 4
SSD_NB = 2

NEG_INF = float("-inf")


def _params(*sem):
    return pltpu.CompilerParams(dimension_semantics=sem, vmem_limit_bytes=VMEM_LIMIT)


def _sigmoid(x):
    return 1.0 / (1.0 + jnp.exp(-x))


def _silu(x):
    return x * _sigmoid(x)


def _softplus(x):
    return jnp.maximum(x, 0.0) + jnp.log(1.0 + jnp.exp(-jnp.abs(x)))


def _log_sigmoid(x):
    return -_softplus(-x)


def _hi_lo(x):
    hi = x.astype(BF16)
    lo = (x - hi.astype(F32)).astype(BF16)
    return hi, lo


def _dot(a, b):
    return jnp.dot(a.astype(BF16), b.astype(BF16), preferred_element_type=F32)


def _dot_nt(a, b):
    return lax.dot_general(a.astype(BF16), b.astype(BF16), (((1,), (1,)), ((), ())),
                           preferred_element_type=F32)


def _norm_mod(x, g, sc, sh):
    y = x * lax.rsqrt(jnp.mean(x * x, axis=-1, keepdims=True) + EPS)
    return (y * g) * (1.0 + sc) + sh


def _mod_kernel(c_ref, w_ref, b_ref, o_ref):
    s = _silu(c_ref[...])
    hi, lo = _hi_lo(s)
    w = w_ref[0]
    whi, wlo = _hi_lo(w)
    acc = jnp.dot(hi, whi, preferred_element_type=F32)
    acc += jnp.dot(lo, whi, preferred_element_type=F32)
    acc += jnp.dot(hi, wlo, preferred_element_type=F32)
    o_ref[0] = acc + b_ref[0]


def _modulation(cc, mod_w, mod_b):
    depth, d, n = mod_w.shape
    rows = cc.shape[0]
    tn = 1536
    return pl.pallas_call(
        _mod_kernel,
        grid=(depth, n // tn),
        in_specs=[pl.BlockSpec((rows, d), lambda l, j: (0, 0)),
                  pl.BlockSpec((1, d, tn), lambda l, j: (l, 0, j)),
                  pl.BlockSpec((1, 1, tn), lambda l, j: (l, 0, j))],
        out_specs=pl.BlockSpec((1, rows, tn), lambda l, j: (l, 0, j)),
        out_shape=jax.ShapeDtypeStruct((depth, rows, n), F32),
        compiler_params=_params("parallel", "parallel"),
        name="modulation",
    )(cc, mod_w, mod_b.reshape(depth, 1, n))


def _norm_proj_kernel(x_ref, g_ref, sc_ref, sh_ref, w_ref, o_ref):
    hb = _norm_mod(x_ref[0], g_ref[...], sc_ref[0], sh_ref[0]).astype(BF16)
    n = w_ref.shape[1]
    for s in range(0, n, 1024):
        e = min(s + 1024, n)
        o_ref[0, :, s:e] = jnp.dot(hb, w_ref[:, s:e], preferred_element_type=F32)


def _bvec_spec(v):
    d = v.shape[-1]
    if v.shape[0] == 1:
        return pl.BlockSpec((1, 1, d), lambda b, t: (0, 0, 0))
    return pl.BlockSpec((1, 1, d), lambda b, t: (b, 0, 0))


def _norm_proj(x, g, sc, sh, w, tm):
    bsz, t, d = x.shape
    n = w.shape[1]
    return pl.pallas_call(
        _norm_proj_kernel,
        grid=(bsz, t // tm),
        in_specs=[pl.BlockSpec((1, tm, d), lambda b, i: (b, i, 0)),
                  pl.BlockSpec((1, d), lambda b, i: (0, 0)),
                  _bvec_spec(sc), _bvec_spec(sh),
                  pl.BlockSpec((d, n), lambda b, i: (0, 0))],
        out_specs=pl.BlockSpec((1, tm, n), lambda b, i: (b, i, 0)),
        out_shape=jax.ShapeDtypeStruct((bsz, t, n), F32),
        compiler_params=_params("parallel", "parallel"),
        name="norm_proj",
    )(x, g.reshape(1, d), sc, sh, w)


def _dwconv_silu_kernel(x_ref, w_ref, b_ref, o_ref):
    x = x_ref[0]
    t = x.shape[0]
    tok = lax.broadcasted_iota(jnp.int32, x.shape, 0)
    prev = jnp.where(tok == 0, 0.0, pltpu.roll(x, 1, 0))
    nxt = jnp.where(tok == t - 1, 0.0, pltpu.roll(x, t - 1, 0))
    y = prev * w_ref[0:1] + x * w_ref[1:2] + nxt * w_ref[2:3] + b_ref[...]
    o_ref[0] = _silu(y).astype(o_ref.dtype)


def _dwconv_silu(x, col0, w, b):
    bsz, t, _ = x.shape
    ch = w.shape[1]
    ct = 256
    off = col0 // ct
    return pl.pallas_call(
        _dwconv_silu_kernel,
        grid=(bsz, ch // ct),
        in_specs=[pl.BlockSpec((1, t, ct), lambda b_, j: (b_, 0, off + j)),
                  pl.BlockSpec((3, ct), lambda b_, j: (0, j)),
                  pl.BlockSpec((1, ct), lambda b_, j: (0, j))],
        out_specs=pl.BlockSpec((1, t, ct), lambda b_, j: (b_, 0, j)),
        out_shape=jax.ShapeDtypeStruct((bsz, t, ch), BF16),
        compiler_params=_params("parallel", "parallel"),
        name="dwconv_silu",
    )(x, w, b.reshape(1, ch))


def _flip2(a):
    return a[..., ::-1, ::-1].copy()


GLA_LEVELS = (32, 16, 8, 4, 2, 1)
GLA_U_ROWS = 14 * GLA_CHUNK + SUBLANES


def _gla_constants():
    L = GLA_CHUNK
    t = np.arange(L)[:, None]
    j = np.arange(L)[None, :]
    blocks = [(j <= t), (j > t)]
    uq, uk, masks = [], [], [np.eye(L)]
    for m in GLA_LEVELS:
        pos, blk = t % (2 * m), t // (2 * m)
        ref = blk * 2 * m + m - 1
        uq.append((pos >= m) & (j > ref) & (j <= t))
        uk.append((pos < m) & (j > t) & (j <= ref))
        masks.append((blk == blk.T) & (pos >= m) & (pos.T < m))
    blocks = [np.asarray(b_, np.float32) for b_ in blocks + uq + uk]
    masks = [np.asarray(m_, np.float32) for m_ in masks]
    us, mask2s = [], []
    for reverse in (False, True):
        bl = [_flip2(b_) for b_ in blocks] if reverse else blocks
        ms = [_flip2(m_) for m_ in masks] if reverse else masks
        us.append(np.concatenate(bl + [np.ones((SUBLANES, L), np.float32)], axis=0))
        mask2s.append(np.stack([np.concatenate([m_, m_], axis=0) for m_ in ms]))
    return jnp.asarray(np.stack(us), BF16), jnp.asarray(np.stack(mask2s), F32)


def _tri_pair(L):
    t = np.arange(L)[:, None]
    j = np.arange(L)[None, :]
    tri = np.stack([np.asarray(j <= t, np.float32), np.asarray(j >= t, np.float32)])
    return jnp.asarray(tri, BF16), jnp.asarray(tri.transpose(0, 2, 1), BF16)


def _const_spec(a):
    return pl.BlockSpec(a.shape, lambda b, c: (0,) * a.ndim)


def _state_spec(a, nb):
    return pl.BlockSpec((nb,) + a.shape[1:], lambda b, c: (b,) + (0,) * (a.ndim - 1))


def _seq_specs(nb, L, nc):
    fwd = lambda w_, i: pl.BlockSpec((nb, L, w_), lambda b, c: (b, c, i))
    bwd = lambda w_, i: pl.BlockSpec((nb, L, w_), lambda b, c: (b, nc - 1 - c, i))
    return fwd, bwd


def _chunk_specs(nb, rows, L, nc):
    fwd = pl.BlockSpec((nb, 1, rows, L), lambda b, c: (b, c, 0, 0))
    bwd = pl.BlockSpec((nb, 1, rows, L), lambda b, c: (b, nc - 1 - c, 0, 0))
    return fwd, bwd


def _round_robin(chains):
    chains = list(chains)
    while chains:
        alive = []
        for ch in chains:
            try:
                next(ch)
                alive.append(ch)
            except StopIteration:
                pass
        chains = alive


def _gla_tile(q, k, vj, e, mask_ref, d, st_ref, o_ref, first, same_head):
    L = GLA_CHUNK
    cum, e_end, c_end = e[0:L], e[L:2 * L], e[14 * L:14 * L + 1]
    q = q * (GLA_DK ** -0.5)
    vj_t = vj.T
    q_in = q * jnp.exp(cum)
    k_end = k * jnp.exp(e_end)
    st = st_ref[...]
    o_inter = _dot_nt(q_in, st)
    att = jnp.zeros((2 * L, L), F32)
    for lvl in range(7):
        ql = q if lvl == 0 else q * jnp.exp(e[(1 + lvl) * L:(2 + lvl) * L])
        kl = k if lvl == 0 else k * jnp.exp(e[(7 + lvl) * L:(8 + lvl) * L])
        lhs = jnp.concatenate([jnp.where(first, ql, 0.0), jnp.where(first, 0.0, ql)], axis=0)
        att = att + _dot_nt(lhs, kl) * mask_ref[d, lvl]
        yield
    o_a = _dot(att[:L], vj[:, :GLA_DV])
    o_b = _dot(att[L:], vj[:, GLA_DV:])
    upd = _dot(vj_t, k_end)
    yield
    o_ref[...] = jnp.concatenate([o_a, o_b], axis=1) + o_inter
    st_ref[...] = st * jnp.exp(c_end) + jnp.where(same_head, upd, 0.0)
    yield


def _gla_kernel(qf, kf, vf, lrf, qb, kb, vb, lrb, a2_ref, ab_ref, u_ref, mask_ref, s0_ref,
                of_ref, ob_ref, s_out_ref, s_scr, *, n_chunks, nb):
    c = pl.program_id(1)

    @pl.when(c == 0)
    def _():
        s_scr[...] = s0_ref[...]

    first = lax.broadcasted_iota(jnp.int32, (GLA_CHUNK, LANES), 1) < GLA_DK
    row = lax.broadcasted_iota(jnp.int32, (2 * GLA_DV, LANES), 0)
    lane = lax.broadcasted_iota(jnp.int32, (2 * GLA_DV, LANES), 1)
    same_head = (row < GLA_DV) == (lane < GLA_DK)
    sides = ((qf, kf, vf, lrf, of_ref), (qb, kb, vb, lrb, ob_ref))
    samples = [(d, bi) for d in range(2) for bi in range(nb)]
    las = [_log_sigmoid(_dot(sides[d][3][bi], a2_ref[d]) + ab_ref[d]) * (1.0 / GLA_TAU)
           for d, bi in samples]
    es = []
    for (d, bi), la in zip(samples, las):
        hi, lo = _hi_lo(la)
        e2 = jnp.dot(u_ref[d], jnp.concatenate([hi, lo], axis=1), preferred_element_type=F32)
        es.append(e2[:, :GLA_QK] + e2[:, GLA_QK:])
    chains = []
    for (d, bi), e in zip(samples, es):
        q_ref, k_ref, v_ref, _, o_ref = sides[d]
        for j in range(2):
            sl = slice(j * LANES, (j + 1) * LANES)
            vsl = slice(2 * j * GLA_DV, (2 * j + 2) * GLA_DV)
            chains.append(_gla_tile(q_ref[bi, :, sl], k_ref[bi, :, sl], v_ref[bi, :, vsl], e[:, sl], mask_ref,
                                    d, s_scr.at[bi, d, j], o_ref.at[bi, :, vsl], first, same_head))
    _round_robin(chains)

    @pl.when(c == n_chunks - 1)
    def _():
        s_out_ref[...] = s_scr[...]


def _gla_scan(proj, a2p, ab, s0):
    bsz, t, _ = proj.shape
    L, nb = GLA_CHUNK, GLA_NB
    nc = t // L
    u, mask2 = _gla_constants()
    fwd, bwd = _seq_specs(nb, L, nc)
    seqs = lambda s: [s(GLA_QK, 0), s(GLA_QK, 1), s(GLA_V, 1), s(LANES, 28)]
    return pl.pallas_call(
        functools.partial(_gla_kernel, n_chunks=nc, nb=nb),
        grid=(bsz // nb, nc),
        in_specs=seqs(fwd) + seqs(bwd) + [_const_spec(a2p), _const_spec(ab), _const_spec(u),
                                          _const_spec(mask2), _state_spec(s0, nb)],
        out_specs=[fwd(GLA_V, 0), bwd(GLA_V, 0), _state_spec(s0, nb)],
        out_shape=[jax.ShapeDtypeStruct((bsz, t, GLA_V), F32)] * 2 + [jax.ShapeDtypeStruct(s0.shape, F32)],
        scratch_shapes=[pltpu.VMEM((nb,) + s0.shape[1:], F32)],
        compiler_params=_params("parallel", "arbitrary"),
        name="gla_scan",
    )(*([proj] * 8), a2p, ab, u, mask2, s0)


def _mlstm_rows(gr, gbr, trit_b):
    gates_r = gr + gbr
    hi, lo = _hi_lo(_log_sigmoid(gates_r))
    b2 = jnp.dot(jnp.concatenate([hi, lo], axis=0), trit_b, preferred_element_type=F32)
    return gates_r, b2[:16] + b2[16:]


def _running_max(g_r, reverse):
    L = g_r.shape[1]
    x = jnp.concatenate([g_r, jnp.full((1, LANES - L), NEG_INF, F32)], axis=1)
    sh = 1
    while sh < L:
        x = jnp.maximum(x, pltpu.roll(x, LANES - sh if reverse else sh, 1))
        sh *= 2
    return x[:, :L]


def _diag_hi_lo(row, eye2):
    hi, lo = _hi_lo(row)
    return jnp.where(eye2, jnp.concatenate([hi, lo], axis=1).astype(F32), 0.0)


def _mlstm_head(q, k, v, rows, tri, eye2, ones_b, d, h, cn_ref, m_ref, o_ref):
    L = ML_CHUNK
    gates_r, b_rows = rows
    end = 0 if d else L - 1
    ci_, cf_ = 8 * d + h, 8 * d + 4 + h
    b_r, li_r = b_rows[cf_:cf_ + 1, :], gates_r[ci_:ci_ + 1, :]
    b_end = b_r[:, end:end + 1]
    m = m_ref[h][:, 0:1]
    cn = cn_ref[h]
    qk = _dot_nt(q, k) * (ML_D ** -0.5)
    q_cn = _dot(q, cn)
    g_r = li_r - b_r
    mt_r = b_r + jnp.maximum(m, _running_max(g_r, bool(d)))
    d1_r = b_r - mt_r
    lhs = jnp.concatenate([_diag_hi_lo(d1_r, eye2), _diag_hi_lo(mt_r, eye2)], axis=0)
    m_new = jnp.maximum(b_end + m, jnp.max(b_end + g_r, axis=1, keepdims=True))
    decay = jnp.exp(b_end + m - m_new)
    kw_t = k.astype(F32).T * (jnp.exp(b_end + g_r - m_new) * (ML_D ** -0.5))
    v1 = jnp.concatenate([v.astype(BF16), ones_b[:L]], axis=1)
    yield
    cols = jnp.dot(lhs.astype(BF16), ones_b, preferred_element_type=F32)
    yield
    d1_c, mt_c = cols[:L], cols[L:]
    sc = qk * jnp.exp(jnp.where(tri, d1_c[:, :L] + g_r, NEG_INF))
    s_v = jnp.dot(sc.astype(BF16), v1, preferred_element_type=F32)
    yield
    aw = jnp.exp(d1_c + m)
    num = aw * q_cn[:, :ML_D] + s_v[:, :ML_D]
    den = aw * q_cn[:, ML_D:] + s_v[:, ML_D:]
    o_ref[...] = num / jnp.maximum(jnp.abs(den), jnp.exp(-mt_c))
    cn_ref[h] = decay * cn + jnp.dot(kw_t.astype(BF16), v1, preferred_element_type=F32)
    m_ref[h] = jnp.broadcast_to(m_new, (1, LANES))
    yield


def _mlstm_kernel(qf, kf, vf, grf, qb, kb, vb, grb, gbr_ref, tri_ref, trit_ref, cn0_ref, m0_ref,
                  of_ref, ob_ref, cn_out_ref, m_out_ref, cn_scr, m_scr, *, n_chunks, nb):
    c = pl.program_id(1)
    L = ML_CHUNK

    @pl.when(c == 0)
    def _():
        cn_scr[...] = cn0_ref[...]
        m_scr[...] = m0_ref[...]

    ones_b = jnp.ones((2 * L, LANES), BF16)
    eye2 = (lax.broadcasted_iota(jnp.int32, (L, 2 * L), 1) % L) == lax.broadcasted_iota(jnp.int32, (L, 2 * L), 0)
    chains = []
    for d, (q_ref, k_ref, v_ref, gr_ref, o_ref) in enumerate(((qf, kf, vf, grf, of_ref),
                                                              (qb, kb, vb, grb, ob_ref))):
        tri = tri_ref[d] > 0.5
        for bi in range(nb):
            rows = _mlstm_rows(gr_ref[bi, 0], gbr_ref[...], trit_ref[d])
            for h in range(ML_HEADS):
                sl = slice(h * ML_D, (h + 1) * ML_D)
                chains.append(_mlstm_head(q_ref[bi, :, sl], k_ref[bi, :, sl], v_ref[bi, :, sl], rows, tri, eye2,
                                          ones_b, d, h, cn_scr.at[bi, d], m_scr.at[bi, d], o_ref.at[bi, :, sl]))
    _round_robin(chains)

    @pl.when(c == n_chunks - 1)
    def _():
        cn_out_ref[...] = cn_scr[...]
        m_out_ref[...] = m_scr[...]


def _mlstm_scan(qk, proj, gates_r, gate_b, state):
    bsz, t, _ = proj.shape
    L, nb = ML_CHUNK, ML_NB
    nc = t // L
    tri_b, trit_b = _tri_pair(L)
    gbr = gate_b.reshape(16, 1)
    fwd, bwd = _seq_specs(nb, L, nc)
    gfwd, gbwd = _chunk_specs(nb, 16, L, nc)
    seqs = lambda s, g: [s(ML_W, 0), s(ML_W, 1), s(ML_W, 5), g]
    st_specs = [_state_spec(a, nb) for a in state]
    outs = pl.pallas_call(
        functools.partial(_mlstm_kernel, n_chunks=nc, nb=nb),
        grid=(bsz // nb, nc),
        in_specs=seqs(fwd, gfwd) + seqs(bwd, gbwd) + [_const_spec(gbr), _const_spec(tri_b),
                                                      _const_spec(trit_b)] + st_specs,
        out_specs=[fwd(ML_W, 0), bwd(ML_W, 0)] + st_specs,
        out_shape=[jax.ShapeDtypeStruct((bsz, t, ML_W), F32)] * 2
                  + [jax.ShapeDtypeStruct(a.shape, F32) for a in state],
        scratch_shapes=[pltpu.VMEM((nb,) + a.shape[1:], F32) for a in state],
        compiler_params=_params("parallel", "arbitrary"),
        name="mlstm_scan",
    )(*([qk, qk, proj, gates_r] * 2), gbr, tri_b, trit_b, *state)
    return outs[0], outs[1], tuple(outs[2:])


def _ssd_steps(dtc, dtr, bias_c, bias_r, a_c, a_r, tri_b, trit_b):
    dt_c = _softplus(dtc + bias_c)
    dt_r = _softplus(dtr + bias_r)
    hi, lo = _hi_lo(dt_c * -jnp.exp(a_c))
    c2 = jnp.dot(tri_b, jnp.concatenate([hi, lo], axis=1), preferred_element_type=F32)
    cum_c = c2[:, :LANES] + c2[:, LANES:]
    hi, lo = _hi_lo(dt_r * -jnp.exp(a_r))
    c2 = jnp.dot(jnp.concatenate([hi, lo], axis=0), trit_b, preferred_element_type=F32)
    cum_r = c2[:SSD_HEADS * 2] + c2[SSD_HEADS * 2:]
    return dt_r, cum_c, cum_r


def _ssd_group(x_ref, bg, cg, steps, tri, d, g, s_ref, y_ref, first, first_b):
    L = SSD_CHUNK
    dt_r, cum_c, cum_r = steps
    end = 0 if d else L - 1
    bg_t = bg.astype(F32).T
    if y_ref is not None:
        cb = _dot_nt(cg, bg)
        y_inter = _dot(cg, s_ref[...])
        yield
    for j in range(SSD_HPG // 2):
        ha = d * SSD_HEADS + g * SSD_HPG + 2 * j
        tile = slice(j * LANES, (j + 1) * LANES)
        xt = x_ref[:, tile]
        x2 = jnp.concatenate([xt * first_b, xt * (1 - first_b)], axis=0)
        ws, es, lhs_s, decs = [], [], [], []
        for hh in (ha, ha + 1):
            cum_row, dt_row = cum_r[hh:hh + 1, :], dt_r[hh:hh + 1, :]
            c_end = cum_row[:, end:end + 1]
            lhs_s.append(bg_t * (jnp.exp(c_end - cum_row) * dt_row))
            decs.append(jnp.exp(c_end))
            if y_ref is not None:
                cum_col = jnp.broadcast_to(cum_c[:, hh:hh + 1], (L, LANES))
                ws.append(jnp.exp(jnp.where(tri, cum_col - cum_row, NEG_INF)) * cb * dt_row)
                es.append(jnp.exp(cum_col))
        if y_ref is not None:
            y_ref[:, tile] = (_dot(jnp.concatenate(ws, axis=1), x2)
                              + jnp.where(first, es[0], es[1]) * y_inter[:, tile]).astype(y_ref.dtype)
        dec = jnp.where(first[0:1], decs[0], decs[1])
        s_ref[:, tile] = dec * s_ref[:, tile] + _dot(jnp.concatenate(lhs_s, axis=1), x2)
        yield


def _ssd_kernel(xf, bmf, cmf, dtcf, dtrf, xb, bmb, cmb, dtcb, dtrb, bias_c_ref, bias_r_ref,
                a_c_ref, a_r_ref, tri_ref, trit_ref, s0_ref, *rest, n_chunks, nb, with_out):
    yf_ref, yb_ref = (rest[0], rest[1]) if with_out else (None, None)
    s_out_ref, s_scr = rest[-2:]
    c = pl.program_id(1)

    @pl.when(c == 0)
    def _():
        s_scr[...] = s0_ref[...]

    first = lax.broadcasted_iota(jnp.int32, (SSD_CHUNK, LANES), 1) < SSD_P
    first_b = jnp.where(first, 1.0, 0.0).astype(BF16)
    chains = []
    for d, (x_ref, bm_ref, cm_ref, dtc_ref, dtr_ref, y_ref) in enumerate(
            ((xf, bmf, cmf, dtcf, dtrf, yf_ref), (xb, bmb, cmb, dtcb, dtrb, yb_ref))):
        tri = tri_ref[d] > 0.5
        for bi in range(nb):
            steps = _ssd_steps(dtc_ref[bi], dtr_ref[bi, 0], bias_c_ref[...], bias_r_ref[...], a_c_ref[...],
                               a_r_ref[...], tri_ref[d], trit_ref[d])
            for g in range(SSD_GROUPS):
                nsl = slice(g * SSD_STATE, (g + 1) * SSD_STATE)
                gsl = slice(g * SSD_HPG * SSD_P, (g + 1) * SSD_HPG * SSD_P)
                chains.append(_ssd_group(x_ref.at[bi, :, gsl], bm_ref[bi, :, nsl], cm_ref[bi, :, nsl], steps,
                                         tri, d, g, s_scr.at[bi, d, g],
                                         None if y_ref is None else y_ref.at[bi, :, gsl], first, first_b))
    _round_robin(chains)

    @pl.when(c == n_chunks - 1)
    def _():
        s_out_ref[...] = s_scr[...]


def _ssd_scan(xbc, proj, dt_r, dt_bias, a_log, s0, with_out):
    bsz, t, _ = xbc.shape
    L, nb = SSD_CHUNK, SSD_NB
    nc = t // L
    tri_b, trit_b = _tri_pair(L)
    nh = 2 * SSD_HEADS
    bias_c = jnp.zeros((1, LANES), F32).at[0, :nh].set(dt_bias.reshape(nh))
    a_c = jnp.zeros((1, LANES), F32).at[0, :nh].set(a_log.reshape(nh))
    bias_r, a_r = dt_bias.reshape(nh, 1), a_log.reshape(nh, 1)
    fwd, bwd = _seq_specs(nb, L, nc)
    gfwd, gbwd = _chunk_specs(nb, nh, L, nc)
    nbc = D_INNER // SSD_BC
    seqs = lambda s, g: [s(D_INNER, 0), s(SSD_BC, nbc), s(SSD_BC, nbc + 1), s(LANES, ODD_IN // LANES), g]
    consts = [bias_c, bias_r, a_c, a_r, tri_b, trit_b]
    out_specs, out_shape = [_state_spec(s0, nb)], [jax.ShapeDtypeStruct(s0.shape, F32)]
    if with_out:
        out_specs = [fwd(D_INNER, 0), bwd(D_INNER, 0)] + out_specs
        out_shape = [jax.ShapeDtypeStruct((bsz, t, D_INNER), BF16)] * 2 + out_shape
    outs = pl.pallas_call(
        functools.partial(_ssd_kernel, n_chunks=nc, nb=nb, with_out=with_out),
        grid=(bsz // nb, nc),
        in_specs=seqs(fwd, gfwd) + seqs(bwd, gbwd) + [_const_spec(a) for a in consts] + [_state_spec(s0, nb)],
        out_specs=out_specs, out_shape=out_shape,
        scratch_shapes=[pltpu.VMEM((nb,) + s0.shape[1:], F32)],
        compiler_params=_params("parallel", "arbitrary"),
        name="ssd_scan",
    )(*([xbc, xbc, xbc, proj, dt_r] * 2), *consts, s0)
    return (outs[0], outs[1], outs[2]) if with_out else (None, None, outs[0])


def _head_rms(y, g, width):
    parts = []
    for s in range(0, y.shape[1], width):
        p = y[:, s:s + width]
        parts.append(p * lax.rsqrt(jnp.mean(p * p, axis=-1, keepdims=True) + EPS))
    return jnp.concatenate(parts, axis=1) * g


def _even_out_kernel(ogf_ref, ogb_ref, hmf_ref, hmb_ref, gg_ref, mo_ref, gn_ref, mn_ref, w_ref,
                     x_ref, gate_ref, o_ref):
    gla = _head_rms(ogf_ref[0] + ogb_ref[0], gn_ref[...], GLA_DV) * _silu(gg_ref[0])
    ml = _head_rms(_sigmoid(mo_ref[0]) * (hmf_ref[0] + hmb_ref[0]), mn_ref[...], ML_D)
    y = _dot(gla, w_ref[:GLA_V]) + _dot(ml, w_ref[GLA_V:])
    o_ref[0] = x_ref[0] + gate_ref[0] * y


def _even_out(o_gla, h_ml, proj, gla_norm_g, ml_norm_g, w_out, x, gate, tm):
    bsz, t, d = x.shape
    tok = lambda w_, i: pl.BlockSpec((1, tm, w_), lambda b, s: (b, s, i))
    const = lambda a: pl.BlockSpec(a.shape, lambda b, s: (0,) * a.ndim)
    gn, mn = gla_norm_g.reshape(1, GLA_V), ml_norm_g.reshape(1, ML_W)
    return pl.pallas_call(
        _even_out_kernel,
        grid=(bsz, t // tm),
        in_specs=[tok(GLA_V, 0)] * 2 + [tok(ML_W, 0)] * 2 + [tok(GLA_V, 2), tok(ML_W, 6), const(gn),
                                                             const(mn), const(w_out), tok(d, 0),
                                                             _bvec_spec(gate)],
        out_specs=tok(d, 0),
        out_shape=jax.ShapeDtypeStruct(x.shape, F32),
        compiler_params=_params("parallel", "parallel"),
        name="even_out",
    )(*o_gla, *h_ml, proj, proj, gn, mn, w_out, x, gate)


def _odd_out_kernel(yf_ref, yb_ref, xs_ref, z_ref, dsk_ref, ng_ref, w_ref, x_ref, gate_ref, o_ref):
    y = yf_ref[0].astype(F32) + yb_ref[0].astype(F32) + dsk_ref[...] * xs_ref[0].astype(F32)
    y = y * _silu(z_ref[0])
    y = _head_rms(y, ng_ref[...], D_INNER // SSD_GROUPS)
    o_ref[0] = x_ref[0] + gate_ref[0] * _dot(y, w_ref[...])


def _odd_out(y, xbc, proj, d_skip, norm_g, w_out, x, gate, tm):
    bsz, t, d = x.shape
    tok = lambda w_, i: pl.BlockSpec((1, tm, w_), lambda b, s: (b, s, i))
    const = lambda a: pl.BlockSpec(a.shape, lambda b, s: (0,) * a.ndim)
    dsk = jnp.repeat(d_skip, SSD_P).reshape(1, D_INNER)
    ng = norm_g.reshape(1, D_INNER)
    return pl.pallas_call(
        _odd_out_kernel,
        grid=(bsz, t // tm),
        in_specs=[tok(D_INNER, 0)] * 4 + [const(dsk), const(ng), const(w_out), tok(d, 0), _bvec_spec(gate)],
        out_specs=tok(d, 0),
        out_shape=jax.ShapeDtypeStruct(x.shape, F32),
        compiler_params=_params("parallel", "parallel"),
        name="odd_out",
    )(*y, xbc, proj, dsk, ng, w_out, x, gate)


FFN_TB = 256
FFN_PAD = GRID_W
FFN_CT = 256


def _ffn_shift_matrix(grid_conv):
    i = np.arange(FFN_TB)[:, None]
    j = np.arange(FFN_TB)[None, :]
    prev, nxt = (j == i - 1), (j == i + 1)
    if grid_conv:
        prev &= (i % GRID_W != 0)
        nxt &= (i % GRID_W != GRID_W - 1)
    return jnp.asarray(np.concatenate([prev, nxt], axis=0), BF16)


def _ffn_conv_block(uc_scr, ul_scr, ur_scr, r0, w9, b, grid_conv):
    acc = b.astype(BF16)
    for dr in ((-1, 0, 1) if grid_conv else (0,)):
        rows = pl.ds(FFN_PAD + r0 + GRID_W * dr, FFN_TB)
        i = 3 * (dr + 1)
        acc = acc + (ul_scr[rows, :] * w9[i:i + 1].astype(BF16) + uc_scr[rows, :] * w9[i + 1:i + 2].astype(BF16)
                     + ur_scr[rows, :] * w9[i + 2:i + 3].astype(BF16))
    return acc


def _ffn_up_kernel(x_ref, g_ref, sc_ref, sh_ref, wa_ref, wg_ref, cwa_ref, cwg_ref, cba_ref, cbg_ref, shift_ref,
                   o_ref, h_scr, *u_scrs, grid_conv):
    t = x_ref.shape[1]
    ct = o_ref.shape[2]

    @pl.when(pl.program_id(1) == 0)
    def _():
        h_scr[...] = _norm_mod(x_ref[0], g_ref[...], sc_ref[0], sh_ref[0]).astype(BF16)
        zeros = jnp.zeros((FFN_PAD, ct), BF16)
        for scr in u_scrs:
            scr[pl.ds(0, FFN_PAD), :] = zeros
            scr[pl.ds(FFN_PAD + t, FFN_PAD), :] = zeros

    halves = ((wa_ref, cwa_ref, cba_ref, u_scrs[0:3]), (wg_ref, cwg_ref, cbg_ref, u_scrs[3:6]))

    def up_block(blk):
        hb = h_scr[pl.ds(blk * FFN_TB, FFN_TB), :]
        for w_ref, _, _, (uc_scr, _, _) in halves:
            uc_scr[pl.ds(FFN_PAD + blk * FFN_TB, FFN_TB), :] = jnp.dot(
                hb, w_ref[...], preferred_element_type=F32).astype(BF16)

    def shift_block(blk):
        rows = pl.ds(FFN_PAD + blk * FFN_TB, FFN_TB)
        for _, _, _, (uc_scr, ul_scr, ur_scr) in halves:
            lr = jnp.dot(shift_ref[...], uc_scr[rows, :], preferred_element_type=F32).astype(BF16)
            ul_scr[rows, :] = lr[:FFN_TB]
            ur_scr[rows, :] = lr[FFN_TB:]

    def conv_block(blk):
        r0 = blk * FFN_TB
        a, g = [_ffn_conv_block(*scrs, r0, cw_ref, cb_ref[...], grid_conv).astype(F32)
                for _, cw_ref, cb_ref, scrs in halves]
        o_ref[0, pl.ds(r0, FFN_TB), :] = (_silu(g) * a).astype(BF16)

    nblk = t // FFN_TB
    for step in range(nblk + 3):
        if step < nblk:
            up_block(step)
        if 0 <= step - 1 < nblk:
            shift_block(step - 1)
        if 0 <= step - 3 < nblk:
            conv_block(step - 3)


def _ffn_up(x, g, sc, sh, w_up, conv_w9, conv_b, grid_conv):
    bsz, t, d = x.shape
    ct = FFN_CT
    nct = D_FF // ct
    cb = conv_b.reshape(1, 2 * D_FF)
    assert t % FFN_TB == 0 and (grid_conv or t == FFN_TB)
    shift = _ffn_shift_matrix(grid_conv)
    return pl.pallas_call(
        functools.partial(_ffn_up_kernel, grid_conv=grid_conv),
        grid=(bsz, nct),
        in_specs=[pl.BlockSpec((1, t, d), lambda b, j: (b, 0, 0)),
                  pl.BlockSpec((1, d), lambda b, j: (0, 0)),
                  _bvec_spec(sc), _bvec_spec(sh),
                  pl.BlockSpec((d, ct), lambda b, j: (0, j)),
                  pl.BlockSpec((d, ct), lambda b, j: (0, nct + j)),
                  pl.BlockSpec((9, ct), lambda b, j: (0, j)),
                  pl.BlockSpec((9, ct), lambda b, j: (0, nct + j)),
                  pl.BlockSpec((1, ct), lambda b, j: (0, j)),
                  pl.BlockSpec((1, ct), lambda b, j: (0, nct + j)),
                  pl.BlockSpec(shift.shape, lambda b, j: (0, 0))],
        out_specs=pl.BlockSpec((1, t, ct), lambda b, j: (b, 0, j)),
        out_shape=jax.ShapeDtypeStruct((bsz, t, D_FF), BF16),
        scratch_shapes=[pltpu.VMEM((t, d), BF16)] + [pltpu.VMEM((t + 2 * FFN_PAD, ct), BF16)] * 6,
        compiler_params=_params("parallel", "arbitrary"),
        name="ffn_up",
    )(x, g.reshape(1, d), sc, sh, w_up, w_up, conv_w9, conv_w9, cb, cb, shift)


def _ffn_down_kernel(a_ref, w_ref, x_ref, gate_ref, fg_ref, o_ref, *, final_norm):
    y = x_ref[0] + gate_ref[0] * jnp.dot(a_ref[0], w_ref[...], preferred_element_type=F32)
    if final_norm:
        y = y * lax.rsqrt(jnp.mean(y * y, axis=-1, keepdims=True) + EPS) * fg_ref[...]
    o_ref[0] = y


def _ffn_down(act, w_down, x, gate, final_g, final_norm, tm):
    bsz, t, d = x.shape
    tok = lambda w_: pl.BlockSpec((1, tm, w_), lambda b, s: (b, s, 0))
    const = lambda a: pl.BlockSpec(a.shape, lambda b, s: (0,) * a.ndim)
    fg = final_g.reshape(1, d)
    return pl.pallas_call(
        functools.partial(_ffn_down_kernel, final_norm=final_norm),
        grid=(bsz, t // tm),
        in_specs=[tok(D_FF), const(w_down), tok(d), _bvec_spec(gate), const(fg)],
        out_specs=tok(d),
        out_shape=jax.ShapeDtypeStruct(x.shape, F32),
        compiler_params=_params("parallel", "parallel"),
        name="ffn_down",
    )(act, w_down, x, gate, fg)


def _rows_to_chunks(a, L):
    bsz, t, ch = a.shape
    return a.reshape(bsz, t // L, L, ch).transpose(0, 1, 3, 2)


def _even_mixer(xc, xl, mods_c, mods_l, norm_g, p):
    (sh1c, sc1c, g1c), (sh1l, sc1l, g1l) = mods_c, mods_l
    w_in = jnp.concatenate([p["w_in"], p["a1"][0], p["a1"][1],
                            jnp.zeros((D_MODEL, EVEN_N - EVEN_IN - 2 * GLA_RANK), F32)], axis=1).astype(BF16)
    projs = {"c": _norm_proj(xc, norm_g, sc1c, sh1c, w_in, 256),
             "l": _norm_proj(xl, norm_g, sc1l, sh1l, w_in, 256)}
    mq_off = GLA_QK * 2 + GLA_V * 2
    qks = {s: _dwconv_silu(projs[s], mq_off, p["conv_w"], p["conv_b"]) for s in projs}
    gates_r = {s: _rows_to_chunks(projs[s][:, :, EVEN_IN - 16:EVEN_IN], ML_CHUNK) for s in projs}
    bsz = xl.shape[0]

    a2p = jnp.stack([jnp.zeros((LANES, GLA_QK), F32).at[16 + 16 * d:32 + 16 * d].set(p["a2"][d])
                     for d in range(2)]).astype(BF16)
    ab = p["ab"].reshape(2, 1, GLA_QK)
    gla_state = jnp.zeros((bsz, 2, 2, 2 * GLA_DV, LANES), F32)
    ml_state = (jnp.zeros((bsz, 2, ML_HEADS, ML_D, 2 * ML_D), F32), jnp.zeros((bsz, 2, ML_HEADS, 1, LANES), F32))
    o_gla, h_ml = {}, {}
    for s in ("c", "l"):
        of, ob, gla_state = _gla_scan(projs[s], a2p, ab, gla_state)
        o_gla[s] = (of, ob)
        hf, hb, ml_state = _mlstm_scan(qks[s], projs[s], gates_r[s], p["gate_b"], ml_state)
        h_ml[s] = (hf, hb)

    w_out = p["w_out"].astype(BF16)
    xc = _even_out(o_gla["c"], h_ml["c"], projs["c"], p["gla_norm_g"], p["ml_norm_g"], w_out, xc, g1c, 256)
    xl = _even_out(o_gla["l"], h_ml["l"], projs["l"], p["gla_norm_g"], p["ml_norm_g"], w_out, xl, g1l, 512)
    return xc, xl


def _odd_mixer_last(xc, xl, mods_c, mods_l, norm_g, p):
    (sh1c, sc1c, _), (sh1l, sc1l, g1l) = mods_c, mods_l
    w_in = jnp.concatenate([p["w_in"], jnp.zeros((D_MODEL, ODD_N - ODD_IN), F32)], axis=1).astype(BF16)
    projs = {"c": _norm_proj(xc, norm_g, sc1c, sh1c, w_in, 256),
             "l": _norm_proj(xl, norm_g, sc1l, sh1l, w_in, 256)}
    xbcs = {s: _dwconv_silu(projs[s], D_INNER, p["conv_w"], p["conv_b"]) for s in projs}
    dt_rs = {s: _rows_to_chunks(projs[s][:, :, ODD_IN - 2 * SSD_HEADS:ODD_IN], SSD_CHUNK) for s in projs}
    bsz = xl.shape[0]
    state = jnp.zeros((bsz, 2, SSD_GROUPS, SSD_STATE, SSD_HPG * SSD_P), F32)
    _, _, state = _ssd_scan(xbcs["c"], projs["c"], dt_rs["c"], p["dt_bias"], p["a_log"], state, False)
    yf, yb, _ = _ssd_scan(xbcs["l"], projs["l"], dt_rs["l"], p["dt_bias"], p["a_log"], state, True)
    return _odd_out((yf, yb), xbcs["l"], projs["l"], p["d_skip"], p["norm_g"], p["w_out"].astype(BF16),
                    xl, g1l, 256)


def kernel(x, c, ctx, c_ctx, mod_w, mod_b, norm_mix_g, norm_ffn_g, final_norm_g, ffn_w_up, ffn_conv_w, ffn_conv_b, ffn_w_down, even_w_in, gla_a1, gla_a2, gla_ab, ml_conv_w, ml_conv_b, ml_gate_b, gla_norm_g, ml_norm_g, even_w_out, ssd_w_in, ssd_conv_w, ssd_conv_b, ssd_dt_bias, ssd_a_log, ssd_d, ssd_norm_g, ssd_w_out):
    depth = mod_w.shape[0]
    bsz = x.shape[0]
    assert depth == 2 and x.shape[1] % (GRID_W * SUBLANES) == 0
    assert bsz % GLA_NB == 0 and bsz % ML_NB == 0 and bsz % SSD_NB == 0
    rows = -(-(bsz + 1) // SUBLANES) * SUBLANES
    cc = jnp.zeros((rows, D_MODEL), F32).at[:bsz].set(c).at[bsz].set(c_ctx)
    mod = _modulation(cc, mod_w, mod_b)

    def mods(layer, lo, hi):
        return [mod[layer, lo:hi, i * D_MODEL:(i + 1) * D_MODEL][:, None, :] for i in range(6)]

    xl, xc = x, ctx
    for layer in range(depth):
        last = layer == depth - 1
        sh1l, sc1l, g1l, sh2l, sc2l, g2l = mods(layer, 0, bsz)
        sh1c, sc1c, g1c, sh2c, sc2c, g2c = mods(layer, bsz, bsz + 1)
        if layer == 0:
            p = dict(w_in=even_w_in[0], a1=gla_a1[0], a2=gla_a2[0], ab=gla_ab[0], conv_w=ml_conv_w[0],
                     conv_b=ml_conv_b[0], gate_b=ml_gate_b[0], gla_norm_g=gla_norm_g[0],
                     ml_norm_g=ml_norm_g[0], w_out=even_w_out[0])
            xc, xl = _even_mixer(xc, xl, (sh1c, sc1c, g1c), (sh1l, sc1l, g1l), norm_mix_g[layer], p)
        else:
            p = dict(w_in=ssd_w_in[0], conv_w=ssd_conv_w[0], conv_b=ssd_conv_b[0], dt_bias=ssd_dt_bias[0],
                     a_log=ssd_a_log[0], d_skip=ssd_d[0], norm_g=ssd_norm_g[0], w_out=ssd_w_out[0])
            xl = _odd_mixer_last(xc, xl, (sh1c, sc1c, g1c), (sh1l, sc1l, g1l), norm_mix_g[layer], p)
        w_up = ffn_w_up[layer].astype(BF16)
        w_down = ffn_w_down[layer].astype(BF16)
        cw9 = ffn_conv_w[layer].reshape(9, 2 * D_FF)
        act = _ffn_up(xl, norm_ffn_g[layer], sc2l, sh2l, w_up, cw9, ffn_conv_b[layer], True)
        xl = _ffn_down(act, w_down, xl, g2l, final_norm_g, last, 512)
        if not last:
            act = _ffn_up(xc, norm_ffn_g[layer], sc2c, sh2c, w_up, cw9, ffn_conv_b[layer], False)
            xc = _ffn_down(act, w_down, xc, g2c, final_norm_g, False, 256)
    return xl
```

```python
import functools

import numpy as np
import jax
import jax.numpy as jnp
from jax import lax
from jax.experimental import pallas as pl
from jax.experimental.pallas import tpu as pltpu

F32 = jnp.float32
BF16 = jnp.bfloat16

D_MODEL = 1024
GRID_W = 64
EPS = 1e-6

GLA_HEADS, GLA_DK, GLA_DV, GLA_RANK, GLA_TAU, GLA_CHUNK = 4, 64, 128, 16, 16.0, 64
ML_HEADS, ML_D, ML_CHUNK = 4, 128, 64
D_INNER, SSD_HEADS, SSD_GROUPS, SSD_HPG, SSD_P, SSD_STATE, SSD_CHUNK = 2048, 32, 4, 8, 64, 128, 128
D_FF = 2816
GLA_QK, GLA_V, ML_W = GLA_HEADS * GLA_DK, GLA_HEADS * GLA_DV, ML_HEADS * ML_D
EVEN_IN = 3600
ODD_IN = 5184
SSD_BC = SSD_GROUPS * SSD_STATE
SSD_CONV_CH = D_INNER + 2 * SSD_BC

LANES = 128
SUBLANES = 8
EVEN_N = 3712
ODD_N = 5248
V7X_VMEM_BYTES = 64 * 1024 * 1024
VMEM_LIMIT = V7X_VMEM_BYTES - 8 * 1024 * 1024

GLA_NB = 4
ML_NB = 4
SSD_NB = 2

NEG_INF = float("-inf")
LOG2_E = 1.4426950408889634


def _params(*sem):
    return pltpu.CompilerParams(dimension_semantics=sem, vmem_limit_bytes=VMEM_LIMIT)


def _sigmoid(x):
    return 1.0 / (1.0 + jnp.exp(-x))


def _silu(x):
    return x * _sigmoid(x)


def _softplus(x):
    return jnp.maximum(x, 0.0) + jnp.log(1.0 + jnp.exp(-jnp.abs(x)))


def _log_sigmoid(x):
    return -_softplus(-x)


def _hi_lo(x):
    hi = x.astype(BF16)
    lo = (x - hi.astype(F32)).astype(BF16)
    return hi, lo


def _dot(a, b):
    return jnp.dot(a.astype(BF16), b.astype(BF16), preferred_element_type=F32)


def _dot_nt(a, b):
    return lax.dot_general(a.astype(BF16), b.astype(BF16), (((1,), (1,)), ((), ())),
                           preferred_element_type=F32)


def _norm_mod(x, g, sc, sh):
    y = x * lax.rsqrt(jnp.mean(x * x, axis=-1, keepdims=True) + EPS)
    return (y * g) * (1.0 + sc) + sh


def _mod_kernel(c_ref, w_ref, b_ref, o_ref):
    s = _silu(c_ref[...])
    hi, lo = _hi_lo(s)
    w = w_ref[0]
    whi, wlo = _hi_lo(w)
    acc = jnp.dot(hi, whi, preferred_element_type=F32)
    acc += jnp.dot(lo, whi, preferred_element_type=F32)
    acc += jnp.dot(hi, wlo, preferred_element_type=F32)
    o_ref[0] = acc + b_ref[0]


def _modulation(cc, mod_w, mod_b):
    depth, d, n = mod_w.shape
    rows = cc.shape[0]
    tn = 1536
    return pl.pallas_call(
        _mod_kernel,
        grid=(depth, n // tn),
        in_specs=[pl.BlockSpec((rows, d), lambda l, j: (0, 0)),
                  pl.BlockSpec((1, d, tn), lambda l, j: (l, 0, j)),
                  pl.BlockSpec((1, 1, tn), lambda l, j: (l, 0, j))],
        out_specs=pl.BlockSpec((1, rows, tn), lambda l, j: (l, 0, j)),
        out_shape=jax.ShapeDtypeStruct((depth, rows, n), F32),
        compiler_params=_params("parallel", "parallel"),
        name="modulation",
    )(cc, mod_w, mod_b.reshape(depth, 1, n))


def _norm_proj_kernel(x_ref, xp_ref, xn_ref, g_ref, sc_ref, sh_ref, w_ref, cw_ref, cb_ref, o_ref, oc_ref, *,
                      conv_col0):
    i = pl.program_id(1)
    tm = x_ref.shape[1]
    norm = lambda x: _norm_mod(x, g_ref[...], sc_ref[0], sh_ref[0]).astype(BF16)
    hb = norm(x_ref[0])
    hb_halo = jnp.concatenate([hb, norm(xp_ref[0]), norm(xn_ref[0])], axis=0)
    row = lax.broadcasted_iota(jnp.int32, (tm, LANES), 0)
    first_tile, last_tile = i == 0, i == pl.num_programs(1) - 1
    n = w_ref.shape[1]
    ch = oc_ref.shape[2]
    chunk = 1024
    halos = {}

    def conv_slice(s):
        u = o_ref[0, :, conv_col0 + s:conv_col0 + s + LANES]
        uh = halos.pop(s)
        before = jnp.where(first_tile, 0.0, uh[SUBLANES - 1:SUBLANES])
        after = jnp.where(last_tile, 0.0, uh[SUBLANES:SUBLANES + 1])
        prev = jnp.where(row == 0, before, pltpu.roll(u, 1, 0))
        nxt = jnp.where(row == tm - 1, after, pltpu.roll(u, tm - 1, 0))
        cw = cw_ref[:, s:s + LANES]
        y = prev * cw[0:1] + u * cw[1:2] + nxt * cw[2:3] + cb_ref[:, s:s + LANES]
        oc_ref[0, :, s:s + LANES] = _silu(y).astype(oc_ref.dtype)

    pending = list(range(0, ch, LANES))
    for s in range(0, n, chunk):
        e = min(s + chunk, n)
        with_halo = s < conv_col0 + ch and e > conv_col0
        r = jnp.dot(hb_halo if with_halo else hb, w_ref[:, s:e], preferred_element_type=F32)
        o_ref[0, :, s:e] = r[:tm]
        for c in range(max(s, conv_col0), min(e, conv_col0 + ch), LANES):
            halos[c - conv_col0] = r[tm:, c - s:c - s + LANES]
        while pending and conv_col0 + pending[0] + LANES <= s:
            conv_slice(pending.pop(0))
    for s in pending:
        conv_slice(s)


def _bvec_spec(v):
    d = v.shape[-1]
    if v.shape[0] == 1:
        return pl.BlockSpec((1, 1, d), lambda b, t: (0, 0, 0))
    return pl.BlockSpec((1, 1, d), lambda b, t: (b, 0, 0))


def _norm_proj(x, g, sc, sh, w, tm, conv_col0, conv_w, conv_b):
    bsz, t, d = x.shape
    n = w.shape[1]
    ch = conv_w.shape[1]
    hb = tm // SUBLANES
    last = t // SUBLANES - 1
    return pl.pallas_call(
        functools.partial(_norm_proj_kernel, conv_col0=conv_col0),
        grid=(bsz, t // tm),
        in_specs=[pl.BlockSpec((1, tm, d), lambda b, i: (b, i, 0)),
                  pl.BlockSpec((1, SUBLANES, d), lambda b, i: (b, jnp.maximum(i * hb - 1, 0), 0)),
                  pl.BlockSpec((1, SUBLANES, d), lambda b, i: (b, jnp.minimum((i + 1) * hb, last), 0)),
                  pl.BlockSpec((1, d), lambda b, i: (0, 0)),
                  _bvec_spec(sc), _bvec_spec(sh),
                  pl.BlockSpec((d, n), lambda b, i: (0, 0)),
                  pl.BlockSpec((3, ch), lambda b, i: (0, 0)),
                  pl.BlockSpec((1, ch), lambda b, i: (0, 0))],
        out_specs=[pl.BlockSpec((1, tm, n), lambda b, i: (b, i, 0)),
                   pl.BlockSpec((1, tm, ch), lambda b, i: (b, i, 0))],
        out_shape=[jax.ShapeDtypeStruct((bsz, t, n), F32), jax.ShapeDtypeStruct((bsz, t, ch), BF16)],
        compiler_params=_params("parallel", "parallel"),
        name="norm_proj",
    )(x, x, x, g.reshape(1, d), sc, sh, w, conv_w, conv_b.reshape(1, ch))


def _flip2(a):
    return a[..., ::-1, ::-1].copy()


GLA_LEVELS = (32, 16, 8, 4, 2, 1)
GLA_U_ROWS = 14 * GLA_CHUNK + SUBLANES


def _gla_constants():
    L = GLA_CHUNK
    t = np.arange(L)[:, None]
    j = np.arange(L)[None, :]
    blocks = [(j <= t), (j > t)]
    uq, uk, masks = [], [], [np.eye(L)]
    for m in GLA_LEVELS:
        pos, blk = t % (2 * m), t // (2 * m)
        ref = blk * 2 * m + m - 1
        uq.append((pos >= m) & (j > ref) & (j <= t))
        uk.append((pos < m) & (j > t) & (j <= ref))
        masks.append((blk == blk.T) & (pos >= m) & (pos.T < m))
    blocks = [np.asarray(b_, np.float32) for b_ in blocks + uq + uk]
    masks = [np.asarray(m_, np.float32) for m_ in masks]
    us, mask2s = [], []
    for reverse in (False, True):
        bl = [_flip2(b_) for b_ in blocks] if reverse else blocks
        ms = [_flip2(m_) for m_ in masks] if reverse else masks
        us.append(np.concatenate(bl + [np.ones((SUBLANES, L), np.float32)], axis=0))
        mask2s.append(np.stack([np.concatenate([m_, m_], axis=0) for m_ in ms]))
    return jnp.asarray(np.stack(us), BF16), jnp.asarray(np.stack(mask2s), F32)


def _tri_pair(L):
    t = np.arange(L)[:, None]
    j = np.arange(L)[None, :]
    tri = np.stack([np.asarray(j <= t, np.float32), np.asarray(j >= t, np.float32)])
    return jnp.asarray(tri, BF16), jnp.asarray(tri.transpose(0, 2, 1), BF16)


def _const_spec(a):
    return pl.BlockSpec(a.shape, lambda b, c: (0,) * a.ndim)


def _state_spec(a, nb):
    return pl.BlockSpec((nb,) + a.shape[1:], lambda b, c: (b,) + (0,) * (a.ndim - 1))


def _seq_specs(nb, L, nc):
    fwd = lambda w_, i: pl.BlockSpec((nb, L, w_), lambda b, c: (b, c, i))
    bwd = lambda w_, i: pl.BlockSpec((nb, L, w_), lambda b, c: (b, nc - 1 - c, i))
    return fwd, bwd


def _chunk_specs(nb, rows, L, nc):
    fwd = pl.BlockSpec((nb, 1, rows, L), lambda b, c: (b, c, 0, 0))
    bwd = pl.BlockSpec((nb, 1, rows, L), lambda b, c: (b, nc - 1 - c, 0, 0))
    return fwd, bwd


def _round_robin(chains):
    chains = list(chains)
    while chains:
        alive = []
        for ch in chains:
            try:
                next(ch)
                alive.append(ch)
            except StopIteration:
                pass
        chains = alive


def _gla_tile(q, k, vj, e, mask_ref, d, st_ref, o_ref, first, same_head):
    L = GLA_CHUNK
    cum, e_end, c_end = e[0:L], e[L:2 * L], e[14 * L:14 * L + 1]
    q = q * (GLA_DK ** -0.5)
    vj_t = vj.T
    q_in = q * jnp.exp(cum)
    k_end = k * jnp.exp(e_end)
    st = st_ref[...]
    o_inter = _dot_nt(q_in, st)
    att = jnp.zeros((2 * L, L), F32)
    for lvl in range(7):
        ql = q if lvl == 0 else q * jnp.exp(e[(1 + lvl) * L:(2 + lvl) * L])
        kl = k if lvl == 0 else k * jnp.exp(e[(7 + lvl) * L:(8 + lvl) * L])
        lhs = jnp.concatenate([jnp.where(first, ql, 0.0), jnp.where(first, 0.0, ql)], axis=0)
        att = att + _dot_nt(lhs, kl) * mask_ref[d, lvl]
        yield
    o_a = _dot(att[:L], vj[:, :GLA_DV])
    o_b = _dot(att[L:], vj[:, GLA_DV:])
    upd = _dot(vj_t, k_end)
    yield
    o_ref[...] = jnp.concatenate([o_a, o_b], axis=1) + o_inter
    st_ref[...] = st * jnp.exp(c_end) + jnp.where(same_head, upd, 0.0)
    yield


def _gla_kernel(qf, kf, vf, lrf, qb, kb, vb, lrb, a2_ref, ab_ref, u_ref, mask_ref, s0_ref,
                of_ref, ob_ref, s_out_ref, s_scr, *, n_chunks, nb):
    c = pl.program_id(1)

    @pl.when(c == 0)
    def _():
        s_scr[...] = s0_ref[...]

    first = lax.broadcasted_iota(jnp.int32, (GLA_CHUNK, LANES), 1) < GLA_DK
    row = lax.broadcasted_iota(jnp.int32, (2 * GLA_DV, LANES), 0)
    lane = lax.broadcasted_iota(jnp.int32, (2 * GLA_DV, LANES), 1)
    same_head = (row < GLA_DV) == (lane < GLA_DK)
    sides = ((qf, kf, vf, lrf, of_ref), (qb, kb, vb, lrb, ob_ref))
    samples = [(d, bi) for d in range(2) for bi in range(nb)]
    las = [_log_sigmoid(_dot(sides[d][3][bi], a2_ref[d]) + ab_ref[d]) * (1.0 / GLA_TAU)
           for d, bi in samples]
    es = []
    for (d, bi), la in zip(samples, las):
        hi, lo = _hi_lo(la)
        e2 = jnp.dot(u_ref[d], jnp.concatenate([hi, lo], axis=1), preferred_element_type=F32)
        es.append(e2[:, :GLA_QK] + e2[:, GLA_QK:])
    chains = []
    for (d, bi), e in zip(samples, es):
        q_ref, k_ref, v_ref, _, o_ref = sides[d]
        for j in range(2):
            sl = slice(j * LANES, (j + 1) * LANES)
            vsl = slice(2 * j * GLA_DV, (2 * j + 2) * GLA_DV)
            chains.append(_gla_tile(q_ref[bi, :, sl], k_ref[bi, :, sl], v_ref[bi, :, vsl], e[:, sl], mask_ref,
                                    d, s_scr.at[bi, d, j], o_ref.at[bi, :, vsl], first, same_head))
    _round_robin(chains)

    @pl.when(c == n_chunks - 1)
    def _():
        s_out_ref[...] = s_scr[...]


def _gla_scan(proj, a2p, ab, s0):
    bsz, t, _ = proj.shape
    L, nb = GLA_CHUNK, GLA_NB
    nc = t // L
    u, mask2 = _gla_constants()
    fwd, bwd = _seq_specs(nb, L, nc)
    seqs = lambda s: [s(GLA_QK, 0), s(GLA_QK, 1), s(GLA_V, 1), s(LANES, 28)]
    return pl.pallas_call(
        functools.partial(_gla_kernel, n_chunks=nc, nb=nb),
        grid=(bsz // nb, nc),
        in_specs=seqs(fwd) + seqs(bwd) + [_const_spec(a2p), _const_spec(ab), _const_spec(u),
                                          _const_spec(mask2), _state_spec(s0, nb)],
        out_specs=[fwd(GLA_V, 0), bwd(GLA_V, 0), _state_spec(s0, nb)],
        out_shape=[jax.ShapeDtypeStruct((bsz, t, GLA_V), F32)] * 2 + [jax.ShapeDtypeStruct(s0.shape, F32)],
        scratch_shapes=[pltpu.VMEM((nb,) + s0.shape[1:], F32)],
        compiler_params=_params("parallel", "arbitrary"),
        name="gla_scan",
    )(*([proj] * 8), a2p, ab, u, mask2, s0)


def _mlstm_rows(gr, gbr, trit_b):
    gates_r = gr + gbr
    hi, lo = _hi_lo(_log_sigmoid(gates_r))
    b2 = jnp.dot(jnp.concatenate([hi, lo], axis=0), trit_b, preferred_element_type=F32)
    return gates_r, b2[:16] + b2[16:]


def _running_max(g_r, reverse):
    L = g_r.shape[1]
    x = jnp.concatenate([g_r, jnp.full((1, LANES - L), NEG_INF, F32)], axis=1)
    sh = 1
    while sh < L:
        x = jnp.maximum(x, pltpu.roll(x, LANES - sh if reverse else sh, 1))
        sh *= 2
    return x[:, :L]


def _diag_hi_lo(row, eye2):
    hi, lo = _hi_lo(row)
    return jnp.where(eye2, jnp.concatenate([hi, lo], axis=1).astype(F32), 0.0)


def _mlstm_head(q, k, v, rows, tri, eye2, ones_b, d, h, cn_ref, m_ref, o_ref):
    L = ML_CHUNK
    gates_r, b_rows = rows
    end = 0 if d else L - 1
    ci_, cf_ = 8 * d + h, 8 * d + 4 + h
    b_r, li_r = b_rows[cf_:cf_ + 1, :], gates_r[ci_:ci_ + 1, :]
    b_end = b_r[:, end:end + 1]
    m = m_ref[h][:, 0:1]
    cn = cn_ref[h]
    qk = _dot_nt(q, k) * (ML_D ** -0.5)
    q_cn = _dot(q, cn)
    g_r = li_r - b_r
    mt_r = b_r + jnp.maximum(m, _running_max(g_r, bool(d)))
    d1_r = b_r - mt_r
    lhs = jnp.concatenate([_diag_hi_lo(d1_r, eye2), _diag_hi_lo(mt_r, eye2)], axis=0)
    m_new = jnp.maximum(b_end + m, jnp.max(b_end + g_r, axis=1, keepdims=True))
    decay = jnp.exp(b_end + m - m_new)
    kw_t = k.astype(F32).T * (jnp.exp(b_end + g_r - m_new) * (ML_D ** -0.5))
    v1 = jnp.concatenate([v.astype(BF16), ones_b[:L]], axis=1)
    yield
    cols = jnp.dot(lhs.astype(BF16), ones_b, preferred_element_type=F32)
    yield
    d1_c, mt_c = cols[:L], cols[L:]
    sc = qk * jnp.exp(jnp.where(tri, d1_c[:, :L] + g_r, NEG_INF))
    s_v = jnp.dot(sc.astype(BF16), v1, preferred_element_type=F32)
    yield
    aw = jnp.exp(d1_c + m)
    num = aw * q_cn[:, :ML_D] + s_v[:, :ML_D]
    den = aw * q_cn[:, ML_D:] + s_v[:, ML_D:]
    o_ref[...] = num / jnp.maximum(jnp.abs(den), jnp.exp(-mt_c))
    cn_ref[h] = decay * cn + jnp.dot(kw_t.astype(BF16), v1, preferred_element_type=F32)
    m_ref[h] = jnp.broadcast_to(m_new, (1, LANES))
    yield


def _mlstm_kernel(qf, kf, vf, grf, qb, kb, vb, grb, gbr_ref, tri_ref, trit_ref, cn0_ref, m0_ref,
                  of_ref, ob_ref, cn_out_ref, m_out_ref, cn_scr, m_scr, *, n_chunks, nb):
    c = pl.program_id(1)
    L = ML_CHUNK

    @pl.when(c == 0)
    def _():
        cn_scr[...] = cn0_ref[...]
        m_scr[...] = m0_ref[...]

    ones_b = jnp.ones((2 * L, LANES), BF16)
    eye2 = (lax.broadcasted_iota(jnp.int32, (L, 2 * L), 1) % L) == lax.broadcasted_iota(jnp.int32, (L, 2 * L), 0)
    chains = []
    for d, (q_ref, k_ref, v_ref, gr_ref, o_ref) in enumerate(((qf, kf, vf, grf, of_ref),
                                                              (qb, kb, vb, grb, ob_ref))):
        tri = tri_ref[d] > 0.5
        for bi in range(nb):
            rows = _mlstm_rows(gr_ref[bi, 0], gbr_ref[...], trit_ref[d])
            for h in range(ML_HEADS):
                sl = slice(h * ML_D, (h + 1) * ML_D)
                chains.append(_mlstm_head(q_ref[bi, :, sl], k_ref[bi, :, sl], v_ref[bi, :, sl], rows, tri, eye2,
                                          ones_b, d, h, cn_scr.at[bi, d], m_scr.at[bi, d], o_ref.at[bi, :, sl]))
    _round_robin(chains)

    @pl.when(c == n_chunks - 1)
    def _():
        cn_out_ref[...] = cn_scr[...]
        m_out_ref[...] = m_scr[...]


def _mlstm_scan(qk, proj, gates_r, gate_b, state):
    bsz, t, _ = proj.shape
    L, nb = ML_CHUNK, ML_NB
    nc = t // L
    tri_b, trit_b = _tri_pair(L)
    gbr = gate_b.reshape(16, 1)
    fwd, bwd = _seq_specs(nb, L, nc)
    gfwd, gbwd = _chunk_specs(nb, 16, L, nc)
    seqs = lambda s, g: [s(ML_W, 0), s(ML_W, 1), s(ML_W, 5), g]
    st_specs = [_state_spec(a, nb) for a in state]
    outs = pl.pallas_call(
        functools.partial(_mlstm_kernel, n_chunks=nc, nb=nb),
        grid=(bsz // nb, nc),
        in_specs=seqs(fwd, gfwd) + seqs(bwd, gbwd) + [_const_spec(gbr), _const_spec(tri_b),
                                                      _const_spec(trit_b)] + st_specs,
        out_specs=[fwd(ML_W, 0), bwd(ML_W, 0)] + st_specs,
        out_shape=[jax.ShapeDtypeStruct((bsz, t, ML_W), F32)] * 2
                  + [jax.ShapeDtypeStruct(a.shape, F32) for a in state],
        scratch_shapes=[pltpu.VMEM((nb,) + a.shape[1:], F32) for a in state],
        compiler_params=_params("parallel", "arbitrary"),
        name="mlstm_scan",
    )(*([qk, qk, proj, gates_r] * 2), gbr, tri_b, trit_b, *state)
    return outs[0], outs[1], tuple(outs[2:])


def _ssd_steps(dtc, dtr, bias_c, bias_r, a_c, a_r, tri_b, trit_b):
    dt_c = _softplus(dtc + bias_c)
    dt_r = _softplus(dtr + bias_r)
    hi, lo = _hi_lo(dt_c * (-LOG2_E * jnp.exp(a_c)))
    c2 = jnp.dot(tri_b, jnp.concatenate([hi, lo], axis=1), preferred_element_type=F32)
    cum_c = c2[:, :LANES] + c2[:, LANES:]
    hi, lo = _hi_lo(dt_r * (-LOG2_E * jnp.exp(a_r)))
    c2 = jnp.dot(jnp.concatenate([hi, lo], axis=0), trit_b, preferred_element_type=F32)
    cum_r = c2[:SSD_HEADS * 2] + c2[SSD_HEADS * 2:]
    return dt_r, cum_c, cum_r


def _ssd_group(x_ref, bg, cg, steps, tri, d, g, s_ref, y_ref, first, first_b):
    L = SSD_CHUNK
    dt_r, cum_c, cum_r = steps
    end = 0 if d else L - 1
    bg_t = bg.astype(F32).T
    if y_ref is not None:
        cb = _dot_nt(cg, bg)
        y_inter = _dot(cg, s_ref[...])
        yield
    for j in range(SSD_HPG // 2):
        ha = d * SSD_HEADS + g * SSD_HPG + 2 * j
        tile = slice(j * LANES, (j + 1) * LANES)
        xt = x_ref[:, tile]
        x2 = jnp.concatenate([xt * first_b, xt * (1 - first_b)], axis=0)
        ws, es, lhs_s, decs = [], [], [], []
        for hh in (ha, ha + 1):
            cum_row, dt_row = cum_r[hh:hh + 1, :], dt_r[hh:hh + 1, :]
            c_end = cum_row[:, end:end + 1]
            lhs_s.append(bg_t * (jnp.exp2(c_end - cum_row) * dt_row))
            decs.append(jnp.exp2(c_end))
            if y_ref is not None:
                cum_col = jnp.broadcast_to(cum_c[:, hh:hh + 1], (L, LANES))
                ws.append(jnp.exp2(jnp.where(tri, cum_col - cum_row, NEG_INF)) * cb * dt_row)
                es.append(jnp.exp2(cum_col))
        if y_ref is not None:
            y_ref[:, tile] = (_dot(jnp.concatenate(ws, axis=1), x2)
                              + jnp.where(first, es[0], es[1]) * y_inter[:, tile]).astype(y_ref.dtype)
        dec = jnp.where(first[0:1], decs[0], decs[1])
        s_ref[:, tile] = dec * s_ref[:, tile] + _dot(jnp.concatenate(lhs_s, axis=1), x2)
        yield


def _ssd_kernel(xf, bmf, cmf, dtcf, dtrf, xb, bmb, cmb, dtcb, dtrb, bias_c_ref, bias_r_ref,
                a_c_ref, a_r_ref, tri_ref, trit_ref, s0_ref, *rest, n_chunks, nb, with_out):
    yf_ref, yb_ref = (rest[0], rest[1]) if with_out else (None, None)
    s_out_ref, s_scr = rest[-2:]
    c = pl.program_id(1)

    @pl.when(c == 0)
    def _():
        s_scr[...] = s0_ref[...]

    first = lax.broadcasted_iota(jnp.int32, (SSD_CHUNK, LANES), 1) < SSD_P
    first_b = jnp.where(first, 1.0, 0.0).astype(BF16)
    chains = []
    for d, (x_ref, bm_ref, cm_ref, dtc_ref, dtr_ref, y_ref) in enumerate(
            ((xf, bmf, cmf, dtcf, dtrf, yf_ref), (xb, bmb, cmb, dtcb, dtrb, yb_ref))):
        tri = tri_ref[d] > 0.5
        for bi in range(nb):
            steps = _ssd_steps(dtc_ref[bi], dtr_ref[bi, 0], bias_c_ref[...], bias_r_ref[...], a_c_ref[...],
                               a_r_ref[...], tri_ref[d], trit_ref[d])
            for g in range(SSD_GROUPS):
                nsl = slice(g * SSD_STATE, (g + 1) * SSD_STATE)
                gsl = slice(g * SSD_HPG * SSD_P, (g + 1) * SSD_HPG * SSD_P)
                chains.append(_ssd_group(x_ref.at[bi, :, gsl], bm_ref[bi, :, nsl], cm_ref[bi, :, nsl], steps,
                                         tri, d, g, s_scr.at[bi, d, g],
                                         None if y_ref is None else y_ref.at[bi, :, gsl], first, first_b))
    _round_robin(chains)

    @pl.when(c == n_chunks - 1)
    def _():
        s_out_ref[...] = s_scr[...]


def _ssd_scan(xbc, proj, dt_r, dt_bias, a_log, s0, with_out):
    bsz, t, _ = xbc.shape
    L, nb = SSD_CHUNK, SSD_NB
    nc = t // L
    tri_b, trit_b = _tri_pair(L)
    nh = 2 * SSD_HEADS
    bias_c = jnp.zeros((1, LANES), F32).at[0, :nh].set(dt_bias.reshape(nh))
    a_c = jnp.zeros((1, LANES), F32).at[0, :nh].set(a_log.reshape(nh))
    bias_r, a_r = dt_bias.reshape(nh, 1), a_log.reshape(nh, 1)
    fwd, bwd = _seq_specs(nb, L, nc)
    gfwd, gbwd = _chunk_specs(nb, nh, L, nc)
    nbc = D_INNER // SSD_BC
    seqs = lambda s, g: [s(D_INNER, 0), s(SSD_BC, nbc), s(SSD_BC, nbc + 1), s(LANES, ODD_IN // LANES), g]
    consts = [bias_c, bias_r, a_c, a_r, tri_b, trit_b]
    out_specs, out_shape = [_state_spec(s0, nb)], [jax.ShapeDtypeStruct(s0.shape, F32)]
    if with_out:
        out_specs = [fwd(D_INNER, 0), bwd(D_INNER, 0)] + out_specs
        out_shape = [jax.ShapeDtypeStruct((bsz, t, D_INNER), BF16)] * 2 + out_shape
    outs = pl.pallas_call(
        functools.partial(_ssd_kernel, n_chunks=nc, nb=nb, with_out=with_out),
        grid=(bsz // nb, nc),
        in_specs=seqs(fwd, gfwd) + seqs(bwd, gbwd) + [_const_spec(a) for a in consts] + [_state_spec(s0, nb)],
        out_specs=out_specs, out_shape=out_shape,
        scratch_shapes=[pltpu.VMEM((nb,) + s0.shape[1:], F32)],
        compiler_params=_params("parallel", "arbitrary"),
        name="ssd_scan",
    )(*([xbc, xbc, xbc, proj, dt_r] * 2), *consts, s0)
    return (outs[0], outs[1], outs[2]) if with_out else (None, None, outs[0])


def _head_rms(y, g, width):
    parts = []
    for s in range(0, y.shape[1], width):
        p = y[:, s:s + width]
        parts.append(p * lax.rsqrt(jnp.mean(p * p, axis=-1, keepdims=True) + EPS))
    return jnp.concatenate(parts, axis=1) * g


def _even_out_kernel(ogf_ref, ogb_ref, hmf_ref, hmb_ref, gg_ref, mo_ref, gn_ref, mn_ref, w_ref,
                     x_ref, gate_ref, o_ref):
    gla = _head_rms(ogf_ref[0] + ogb_ref[0], gn_ref[...], GLA_DV) * _silu(gg_ref[0])
    ml = _head_rms(_sigmoid(mo_ref[0]) * (hmf_ref[0] + hmb_ref[0]), mn_ref[...], ML_D)
    y = _dot(gla, w_ref[:GLA_V]) + _dot(ml, w_ref[GLA_V:])
    o_ref[0] = x_ref[0] + gate_ref[0] * y


def _even_out(o_gla, h_ml, proj, gla_norm_g, ml_norm_g, w_out, x, gate, tm):
    bsz, t, d = x.shape
    tok = lambda w_, i: pl.BlockSpec((1, tm, w_), lambda b, s: (b, s, i))
    const = lambda a: pl.BlockSpec(a.shape, lambda b, s: (0,) * a.ndim)
    gn, mn = gla_norm_g.reshape(1, GLA_V), ml_norm_g.reshape(1, ML_W)
    return pl.pallas_call(
        _even_out_kernel,
        grid=(bsz, t // tm),
        in_specs=[tok(GLA_V, 0)] * 2 + [tok(ML_W, 0)] * 2 + [tok(GLA_V, 2), tok(ML_W, 6), const(gn),
                                                             const(mn), const(w_out), tok(d, 0),
                                                             _bvec_spec(gate)],
        out_specs=tok(d, 0),
        out_shape=jax.ShapeDtypeStruct(x.shape, F32),
        compiler_params=_params("parallel", "parallel"),
        name="even_out",
    )(*o_gla, *h_ml, proj, proj, gn, mn, w_out, x, gate)


def _odd_out_kernel(yf_ref, yb_ref, xs_ref, z_ref, dsk_ref, ng_ref, w_ref, x_ref, gate_ref, o_ref):
    y = yf_ref[0].astype(F32) + yb_ref[0].astype(F32) + dsk_ref[...] * xs_ref[0].astype(F32)
    y = y * _silu(z_ref[0])
    y = _head_rms(y, ng_ref[...], D_INNER // SSD_GROUPS)
    o_ref[0] = x_ref[0] + gate_ref[0] * _dot(y, w_ref[...])


def _odd_out(y, xbc, proj, d_skip, norm_g, w_out, x, gate, tm):
    bsz, t, d = x.shape
    tok = lambda w_, i: pl.BlockSpec((1, tm, w_), lambda b, s: (b, s, i))
    const = lambda a: pl.BlockSpec(a.shape, lambda b, s: (0,) * a.ndim)
    dsk = jnp.repeat(d_skip, SSD_P).reshape(1, D_INNER)
    ng = norm_g.reshape(1, D_INNER)
    return pl.pallas_call(
        _odd_out_kernel,
        grid=(bsz, t // tm),
        in_specs=[tok(D_INNER, 0)] * 4 + [const(dsk), const(ng), const(w_out), tok(d, 0), _bvec_spec(gate)],
        out_specs=tok(d, 0),
        out_shape=jax.ShapeDtypeStruct(x.shape, F32),
        compiler_params=_params("parallel", "parallel"),
        name="odd_out",
    )(*y, xbc, proj, dsk, ng, w_out, x, gate)


FFN_TB = 256
FFN_PAD = GRID_W
FFN_CT = 256


def _ffn_shift_matrix(grid_conv):
    i = np.arange(FFN_TB)[:, None]
    j = np.arange(FFN_TB)[None, :]
    prev, nxt = (j == i - 1), (j == i + 1)
    if grid_conv:
        prev &= (i % GRID_W != 0)
        nxt &= (i % GRID_W != GRID_W - 1)
    return jnp.asarray(np.concatenate([prev, nxt], axis=0), BF16)


def _ffn_conv_block(uc_scr, ul_scr, ur_scr, r0, w9, b, grid_conv):
    acc = b.astype(BF16)
    for dr in ((-1, 0, 1) if grid_conv else (0,)):
        rows = pl.ds(FFN_PAD + r0 + GRID_W * dr, FFN_TB)
        i = 3 * (dr + 1)
        acc = acc + (ul_scr[rows, :] * w9[i:i + 1].astype(BF16) + uc_scr[rows, :] * w9[i + 1:i + 2].astype(BF16)
                     + ur_scr[rows, :] * w9[i + 2:i + 3].astype(BF16))
    return acc


def _ffn_up_kernel(x_ref, g_ref, sc_ref, sh_ref, wa_ref, wg_ref, cwa_ref, cwg_ref, cba_ref, cbg_ref, shift_ref,
                   o_ref, h_scr, *u_scrs, grid_conv):
    t = x_ref.shape[1]
    ct = o_ref.shape[3]

    @pl.when(pl.program_id(1) == 0)
    def _():
        h_scr[...] = _norm_mod(x_ref[0], g_ref[...], sc_ref[0], sh_ref[0]).astype(BF16)
        zeros = jnp.zeros((FFN_PAD, ct), BF16)
        for scr in u_scrs:
            scr[pl.ds(0, FFN_PAD), :] = zeros
            scr[pl.ds(FFN_PAD + t, FFN_PAD), :] = zeros

    halves = ((wa_ref, cwa_ref, cba_ref, u_scrs[0:3]), (wg_ref, cwg_ref, cbg_ref, u_scrs[3:6]))

    def up_block(blk):
        hb = h_scr[pl.ds(blk * FFN_TB, FFN_TB), :]
        for w_ref, _, _, (uc_scr, _, _) in halves:
            uc_scr[pl.ds(FFN_PAD + blk * FFN_TB, FFN_TB), :] = jnp.dot(
                hb, w_ref[0], preferred_element_type=F32).astype(BF16)

    def shift_block(blk):
        rows = pl.ds(FFN_PAD + blk * FFN_TB, FFN_TB)
        for _, _, _, (uc_scr, ul_scr, ur_scr) in halves:
            lr = jnp.dot(shift_ref[...], uc_scr[rows, :], preferred_element_type=F32).astype(BF16)
            ul_scr[rows, :] = lr[:FFN_TB]
            ur_scr[rows, :] = lr[FFN_TB:]

    def conv_block(blk):
        r0 = blk * FFN_TB
        a, g = [_ffn_conv_block(*scrs, r0, cw_ref, cb_ref[...], grid_conv).astype(F32)
                for _, cw_ref, cb_ref, scrs in halves]
        o_ref[0, 0, pl.ds(r0, FFN_TB), :] = (_silu(g) * a).astype(BF16)

    nblk = t // FFN_TB
    for step in range(nblk + 3):
        if step < nblk:
            up_block(step)
        if 0 <= step - 1 < nblk:
            shift_block(step - 1)
        if 0 <= step - 3 < nblk:
            conv_block(step - 3)


def _ffn_up(x, g, sc, sh, w_up, conv_w9, conv_b, grid_conv):
    bsz, t, d = x.shape
    ct = FFN_CT
    nct = D_FF // ct
    cb = conv_b.reshape(1, 2 * D_FF)
    assert t % FFN_TB == 0 and (grid_conv or t == FFN_TB)
    shift = _ffn_shift_matrix(grid_conv)
    return pl.pallas_call(
        functools.partial(_ffn_up_kernel, grid_conv=grid_conv),
        grid=(bsz, nct),
        in_specs=[pl.BlockSpec((1, t, d), lambda b, j: (b, 0, 0)),
                  pl.BlockSpec((1, d), lambda b, j: (0, 0)),
                  _bvec_spec(sc), _bvec_spec(sh),
                  pl.BlockSpec((1, d, ct), lambda b, j: (j, 0, 0)),
                  pl.BlockSpec((1, d, ct), lambda b, j: (nct + j, 0, 0)),
                  pl.BlockSpec((9, ct), lambda b, j: (0, j)),
                  pl.BlockSpec((9, ct), lambda b, j: (0, nct + j)),
                  pl.BlockSpec((1, ct), lambda b, j: (0, j)),
                  pl.BlockSpec((1, ct), lambda b, j: (0, nct + j)),
                  pl.BlockSpec(shift.shape, lambda b, j: (0, 0))],
        out_specs=pl.BlockSpec((1, 1, t, ct), lambda b, j: (b, j, 0, 0)),
        out_shape=jax.ShapeDtypeStruct((bsz, nct, t, ct), BF16),
        scratch_shapes=[pltpu.VMEM((t, d), BF16)] + [pltpu.VMEM((t + 2 * FFN_PAD, ct), BF16)] * 6,
        compiler_params=_params("parallel", "arbitrary"),
        name="ffn_up",
    )(x, g.reshape(1, d), sc, sh, w_up, w_up, conv_w9, conv_w9, cb, cb, shift)


def _ffn_down_kernel(a_ref, w_ref, x_ref, gate_ref, fg_ref, o_ref, *, final_norm):
    acc = jnp.dot(a_ref[0, 0], w_ref[0], preferred_element_type=F32)
    for j in range(1, a_ref.shape[1]):
        acc += jnp.dot(a_ref[0, j], w_ref[j], preferred_element_type=F32)
    y = x_ref[0] + gate_ref[0] * acc
    if final_norm:
        y = y * lax.rsqrt(jnp.mean(y * y, axis=-1, keepdims=True) + EPS) * fg_ref[...]
    o_ref[0] = y


def _ffn_down(act, w_down, x, gate, final_g, final_norm, tm):
    bsz, t, d = x.shape
    nct, ct = act.shape[1], act.shape[3]
    tok = lambda w_: pl.BlockSpec((1, tm, w_), lambda b, s: (b, s, 0))
    const = lambda a: pl.BlockSpec(a.shape, lambda b, s: (0,) * a.ndim)
    fg = final_g.reshape(1, d)
    w_down = w_down.reshape(nct, ct, d)
    return pl.pallas_call(
        functools.partial(_ffn_down_kernel, final_norm=final_norm),
        grid=(bsz, t // tm),
        in_specs=[pl.BlockSpec((1, nct, tm, ct), lambda b, s: (b, 0, s, 0)), const(w_down), tok(d),
                  _bvec_spec(gate), const(fg)],
        out_specs=tok(d),
        out_shape=jax.ShapeDtypeStruct(x.shape, F32),
        compiler_params=_params("parallel", "parallel"),
        name="ffn_down",
    )(act, w_down, x, gate, fg)


def _rows_to_chunks(a, L):
    bsz, t, ch = a.shape
    return a.reshape(bsz, t // L, L, ch).transpose(0, 1, 3, 2)


def _even_mixer(xc, xl, mods_c, mods_l, norm_g, p):
    (sh1c, sc1c, g1c), (sh1l, sc1l, g1l) = mods_c, mods_l
    w_in = jnp.concatenate([p["w_in"], p["a1"][0], p["a1"][1],
                            jnp.zeros((D_MODEL, EVEN_N - EVEN_IN - 2 * GLA_RANK), F32)], axis=1).astype(BF16)
    mq_off = GLA_QK * 2 + GLA_V * 2
    projs, qks = {}, {}
    for s, (xs, sc1, sh1) in (("c", (xc, sc1c, sh1c)), ("l", (xl, sc1l, sh1l))):
        projs[s], qks[s] = _norm_proj(xs, norm_g, sc1, sh1, w_in, 256, mq_off, p["conv_w"], p["conv_b"])
    gates_r = {s: _rows_to_chunks(projs[s][:, :, EVEN_IN - 16:EVEN_IN], ML_CHUNK) for s in projs}
    bsz = xl.shape[0]

    a2p = jnp.stack([jnp.zeros((LANES, GLA_QK), F32).at[16 + 16 * d:32 + 16 * d].set(p["a2"][d])
                     for d in range(2)]).astype(BF16)
    ab = p["ab"].reshape(2, 1, GLA_QK)
    gla_state = jnp.zeros((bsz, 2, 2, 2 * GLA_DV, LANES), F32)
    ml_state = (jnp.zeros((bsz, 2, ML_HEADS, ML_D, 2 * ML_D), F32), jnp.zeros((bsz, 2, ML_HEADS, 1, LANES), F32))
    o_gla, h_ml = {}, {}
    for s in ("c", "l"):
        of, ob, gla_state = _gla_scan(projs[s], a2p, ab, gla_state)
        o_gla[s] = (of, ob)
        hf, hb, ml_state = _mlstm_scan(qks[s], projs[s], gates_r[s], p["gate_b"], ml_state)
        h_ml[s] = (hf, hb)

    w_out = p["w_out"].astype(BF16)
    xc = _even_out(o_gla["c"], h_ml["c"], projs["c"], p["gla_norm_g"], p["ml_norm_g"], w_out, xc, g1c, 256)
    xl = _even_out(o_gla["l"], h_ml["l"], projs["l"], p["gla_norm_g"], p["ml_norm_g"], w_out, xl, g1l, 512)
    return xc, xl


def _odd_mixer_last(xc, xl, mods_c, mods_l, norm_g, p):
    (sh1c, sc1c, _), (sh1l, sc1l, g1l) = mods_c, mods_l
    w_in = jnp.concatenate([p["w_in"], jnp.zeros((D_MODEL, ODD_N - ODD_IN), F32)], axis=1).astype(BF16)
    projs, xbcs = {}, {}
    for s, (xs, sc1, sh1) in (("c", (xc, sc1c, sh1c)), ("l", (xl, sc1l, sh1l))):
        projs[s], xbcs[s] = _norm_proj(xs, norm_g, sc1, sh1, w_in, 256, D_INNER, p["conv_w"], p["conv_b"])
    dt_rs = {s: _rows_to_chunks(projs[s][:, :, ODD_IN - 2 * SSD_HEADS:ODD_IN], SSD_CHUNK) for s in projs}
    bsz = xl.shape[0]
    state = jnp.zeros((bsz, 2, SSD_GROUPS, SSD_STATE, SSD_HPG * SSD_P), F32)
    _, _, state = _ssd_scan(xbcs["c"], projs["c"], dt_rs["c"], p["dt_bias"], p["a_log"], state, False)
    yf, yb, _ = _ssd_scan(xbcs["l"], projs["l"], dt_rs["l"], p["dt_bias"], p["a_log"], state, True)
    return _odd_out((yf, yb), xbcs["l"], projs["l"], p["d_skip"], p["norm_g"], p["w_out"].astype(BF16),
                    xl, g1l, 256)


def kernel(x, c, ctx, c_ctx, mod_w, mod_b, norm_mix_g, norm_ffn_g, final_norm_g, ffn_w_up, ffn_conv_w, ffn_conv_b, ffn_w_down, even_w_in, gla_a1, gla_a2, gla_ab, ml_conv_w, ml_conv_b, ml_gate_b, gla_norm_g, ml_norm_g, even_w_out, ssd_w_in, ssd_conv_w, ssd_conv_b, ssd_dt_bias, ssd_a_log, ssd_d, ssd_norm_g, ssd_w_out):
    depth = mod_w.shape[0]
    bsz = x.shape[0]
    assert depth == 2 and x.shape[1] % (GRID_W * SUBLANES) == 0
    assert bsz % GLA_NB == 0 and bsz % ML_NB == 0 and bsz % SSD_NB == 0
    rows = -(-(bsz + 1) // SUBLANES) * SUBLANES
    cc = jnp.zeros((rows, D_MODEL), F32).at[:bsz].set(c).at[bsz].set(c_ctx)
    mod = _modulation(cc, mod_w, mod_b)

    def mods(layer, lo, hi):
        return [mod[layer, lo:hi, i * D_MODEL:(i + 1) * D_MODEL][:, None, :] for i in range(6)]

    xl, xc = x, ctx
    for layer in range(depth):
        last = layer == depth - 1
        sh1l, sc1l, g1l, sh2l, sc2l, g2l = mods(layer, 0, bsz)
        sh1c, sc1c, g1c, sh2c, sc2c, g2c = mods(layer, bsz, bsz + 1)
        if layer == 0:
            p = dict(w_in=even_w_in[0], a1=gla_a1[0], a2=gla_a2[0], ab=gla_ab[0], conv_w=ml_conv_w[0],
                     conv_b=ml_conv_b[0], gate_b=ml_gate_b[0], gla_norm_g=gla_norm_g[0],
                     ml_norm_g=ml_norm_g[0], w_out=even_w_out[0])
            xc, xl = _even_mixer(xc, xl, (sh1c, sc1c, g1c), (sh1l, sc1l, g1l), norm_mix_g[layer], p)
        else:
            p = dict(w_in=ssd_w_in[0], conv_w=ssd_conv_w[0], conv_b=ssd_conv_b[0], dt_bias=ssd_dt_bias[0],
                     a_log=ssd_a_log[0], d_skip=ssd_d[0], norm_g=ssd_norm_g[0], w_out=ssd_w_out[0])
            xl = _odd_mixer_last(xc, xl, (sh1c, sc1c, g1c), (sh1l, sc1l, g1l), norm_mix_g[layer], p)
        w_up = ffn_w_up[layer].astype(BF16).reshape(D_MODEL, 2 * D_FF // FFN_CT, FFN_CT).transpose(1, 0, 2)
        w_down = ffn_w_down[layer].astype(BF16)
        cw9 = ffn_conv_w[layer].reshape(9, 2 * D_FF)
        act = _ffn_up(xl, norm_ffn_g[layer], sc2l, sh2l, w_up, cw9, ffn_conv_b[layer], True)
        xl = _ffn_down(act, w_down, xl, g2l, final_norm_g, last, 512)
        if not last:
            act = _ffn_up(xc, norm_ffn_g[layer], sc2c, sh2c, w_up, cw9, ffn_conv_b[layer], False)
            xc = _ffn_down(act, w_down, xc, g2c, final_norm_g, False, 256)
    return xl
```

```python
import functools

import numpy as np
import jax
import jax.numpy as jnp
from jax import lax
from jax.experimental import pallas as pl
from jax.experimental.pallas import tpu as pltpu

F32 = jnp.float32
BF16 = jnp.bfloat16

D_MODEL = 1024
GRID_W = 64
EPS = 1e-6

GLA_HEADS, GLA_DK, GLA_DV, GLA_RANK, GLA_TAU, GLA_CHUNK = 4, 64, 128, 16, 16.0, 64
ML_HEADS, ML_D, ML_CHUNK = 4, 128, 64
D_INNER, SSD_HEADS, SSD_GROUPS, SSD_HPG, SSD_P, SSD_STATE, SSD_CHUNK = 2048, 32, 4, 8, 64, 128, 128
D_FF = 2816
GLA_QK, GLA_V, ML_W = GLA_HEADS * GLA_DK, GLA_HEADS * GLA_DV, ML_HEADS * ML_D
EVEN_IN = 3600
ODD_IN = 5184
SSD_BC = SSD_GROUPS * SSD_STATE
SSD_CONV_CH = D_INNER + 2 * SSD_BC

LANES = 128
SUBLANES = 8
EVEN_N = 3712
ODD_N = 5248
V7X_VMEM_BYTES = 64 * 1024 * 1024
VMEM_LIMIT = V7X_VMEM_BYTES - 8 * 1024 * 1024

GLA_NB = 4
ML_NB = 4
SSD_NB = 2

NEG_INF = float("-inf")
LOG2_E = 1.4426950408889634


def _params(*sem):
    return pltpu.CompilerParams(dimension_semantics=sem, vmem_limit_bytes=VMEM_LIMIT)


def _sigmoid(x):
    return 1.0 / (1.0 + jnp.exp(-x))


def _silu(x):
    return x * _sigmoid(x)


def _softplus(x):
    return jnp.maximum(x, 0.0) + jnp.log(1.0 + jnp.exp(-jnp.abs(x)))


def _log_sigmoid(x):
    return -_softplus(-x)


def _hi_lo(x):
    hi = x.astype(BF16)
    lo = (x - hi.astype(F32)).astype(BF16)
    return hi, lo


def _dot(a, b):
    return jnp.dot(a.astype(BF16), b.astype(BF16), preferred_element_type=F32)


def _dot_nt(a, b):
    return lax.dot_general(a.astype(BF16), b.astype(BF16), (((1,), (1,)), ((), ())),
                           preferred_element_type=F32)


def _norm_mod(x, g, sc, sh):
    y = x * lax.rsqrt(jnp.mean(x * x, axis=-1, keepdims=True) + EPS)
    return (y * g) * (1.0 + sc) + sh


def _mod_kernel(c_ref, w_ref, b_ref, o_ref):
    s = _silu(c_ref[...])
    hi, lo = _hi_lo(s)
    w = w_ref[0]
    whi, wlo = _hi_lo(w)
    acc = jnp.dot(hi, whi, preferred_element_type=F32)
    acc += jnp.dot(lo, whi, preferred_element_type=F32)
    acc += jnp.dot(hi, wlo, preferred_element_type=F32)
    o_ref[0] = acc + b_ref[0]


def _modulation(cc, mod_w, mod_b):
    depth, d, n = mod_w.shape
    rows = cc.shape[0]
    tn = 1536
    return pl.pallas_call(
        _mod_kernel,
        grid=(depth, n // tn),
        in_specs=[pl.BlockSpec((rows, d), lambda l, j: (0, 0)),
                  pl.BlockSpec((1, d, tn), lambda l, j: (l, 0, j)),
                  pl.BlockSpec((1, 1, tn), lambda l, j: (l, 0, j))],
        out_specs=pl.BlockSpec((1, rows, tn), lambda l, j: (l, 0, j)),
        out_shape=jax.ShapeDtypeStruct((depth, rows, n), F32),
        compiler_params=_params("parallel", "parallel"),
        name="modulation",
    )(cc, mod_w, mod_b.reshape(depth, 1, n))


def _norm_proj_kernel(x_ref, xp_ref, xn_ref, g_ref, sc_ref, sh_ref, w_ref, cw_ref, cb_ref, o_ref, oc_ref, *,
                      conv_col0):
    i = pl.program_id(1)
    tm = x_ref.shape[1]
    norm = lambda x: _norm_mod(x, g_ref[...], sc_ref[0], sh_ref[0]).astype(BF16)
    hb = norm(x_ref[0])
    hb_halo = jnp.concatenate([hb, norm(xp_ref[0]), norm(xn_ref[0])], axis=0)
    row = lax.broadcasted_iota(jnp.int32, (tm, LANES), 0)
    first_tile, last_tile = i == 0, i == pl.num_programs(1) - 1
    n = w_ref.shape[1]
    ch = oc_ref.shape[2]
    chunk = 1024
    halos = {}

    def conv_slice(s):
        u = o_ref[0, :, conv_col0 + s:conv_col0 + s + LANES]
        uh = halos.pop(s)
        before = jnp.where(first_tile, 0.0, uh[SUBLANES - 1:SUBLANES])
        after = jnp.where(last_tile, 0.0, uh[SUBLANES:SUBLANES + 1])
        prev = jnp.where(row == 0, before, pltpu.roll(u, 1, 0))
        nxt = jnp.where(row == tm - 1, after, pltpu.roll(u, tm - 1, 0))
        cw = cw_ref[:, s:s + LANES]
        y = prev * cw[0:1] + u * cw[1:2] + nxt * cw[2:3] + cb_ref[:, s:s + LANES]
        oc_ref[0, :, s:s + LANES] = _silu(y).astype(oc_ref.dtype)

    pending = list(range(0, ch, LANES))
    for s in range(0, n, chunk):
        e = min(s + chunk, n)
        with_halo = s < conv_col0 + ch and e > conv_col0
        r = jnp.dot(hb_halo if with_halo else hb, w_ref[:, s:e], preferred_element_type=F32)
        o_ref[0, :, s:e] = r[:tm]
        for c in range(max(s, conv_col0), min(e, conv_col0 + ch), LANES):
            halos[c - conv_col0] = r[tm:, c - s:c - s + LANES]
        while pending and conv_col0 + pending[0] + LANES <= s:
            conv_slice(pending.pop(0))
    for s in pending:
        conv_slice(s)


def _bvec_spec(v):
    d = v.shape[-1]
    if v.shape[0] == 1:
        return pl.BlockSpec((1, 1, d), lambda b, t: (0, 0, 0))
    return pl.BlockSpec((1, 1, d), lambda b, t: (b, 0, 0))


def _norm_proj(x, g, sc, sh, w, tm, conv_col0, conv_w, conv_b):
    bsz, t, d = x.shape
    n = w.shape[1]
    ch = conv_w.shape[1]
    hb = tm // SUBLANES
    last = t // SUBLANES - 1
    return pl.pallas_call(
        functools.partial(_norm_proj_kernel, conv_col0=conv_col0),
        grid=(bsz, t // tm),
        in_specs=[pl.BlockSpec((1, tm, d), lambda b, i: (b, i, 0)),
                  pl.BlockSpec((1, SUBLANES, d), lambda b, i: (b, jnp.maximum(i * hb - 1, 0), 0)),
                  pl.BlockSpec((1, SUBLANES, d), lambda b, i: (b, jnp.minimum((i + 1) * hb, last), 0)),
                  pl.BlockSpec((1, d), lambda b, i: (0, 0)),
                  _bvec_spec(sc), _bvec_spec(sh),
                  pl.BlockSpec((d, n), lambda b, i: (0, 0)),
                  pl.BlockSpec((3, ch), lambda b, i: (0, 0)),
                  pl.BlockSpec((1, ch), lambda b, i: (0, 0))],
        out_specs=[pl.BlockSpec((1, tm, n), lambda b, i: (b, i, 0)),
                   pl.BlockSpec((1, tm, ch), lambda b, i: (b, i, 0))],
        out_shape=[jax.ShapeDtypeStruct((bsz, t, n), F32), jax.ShapeDtypeStruct((bsz, t, ch), BF16)],
        compiler_params=_params("parallel", "parallel"),
        name="norm_proj",
    )(x, x, x, g.reshape(1, d), sc, sh, w, conv_w, conv_b.reshape(1, ch))


def _flip2(a):
    return a[..., ::-1, ::-1].copy()


GLA_LEVELS = (32, 16, 8, 4, 2, 1)
GLA_U_ROWS = (2 + len(GLA_LEVELS)) * GLA_CHUNK + SUBLANES


def _gla_constants():
    L = GLA_CHUNK
    t = np.arange(L)[:, None]
    j = np.arange(L)[None, :]
    blocks = [(j <= t), (j > t)]
    masks = [np.eye(L)]
    for m in GLA_LEVELS:
        pos, blk = t % (2 * m), t // (2 * m)
        ref = blk * 2 * m + m - 1
        blocks.append(((pos >= m) & (j > ref) & (j <= t)) | ((pos < m) & (j > t) & (j <= ref)))
        masks.append((blk == blk.T) & (pos >= m) & (pos.T < m))
    blocks = [np.asarray(b_, np.float32) for b_ in blocks]
    masks = [np.asarray(m_, np.float32) for m_ in masks]
    us, mask2s = [], []
    for reverse in (False, True):
        bl = [_flip2(b_) for b_ in blocks] if reverse else blocks
        ms = [_flip2(m_) for m_ in masks] if reverse else masks
        us.append(np.concatenate(bl + [np.ones((SUBLANES, L), np.float32)], axis=0))
        mask2s.append(np.stack([np.concatenate([m_, m_], axis=0) for m_ in ms]))
    return jnp.asarray(np.stack(us), BF16), jnp.asarray(np.stack(mask2s), F32)


def _tri_pair(L):
    t = np.arange(L)[:, None]
    j = np.arange(L)[None, :]
    tri = np.stack([np.asarray(j <= t, np.float32), np.asarray(j >= t, np.float32)])
    return jnp.asarray(tri, BF16), jnp.asarray(tri.transpose(0, 2, 1), BF16)


def _const_spec(a):
    return pl.BlockSpec(a.shape, lambda b, c: (0,) * a.ndim)


def _state_spec(a, nb):
    return pl.BlockSpec((nb,) + a.shape[1:], lambda b, c: (b,) + (0,) * (a.ndim - 1))


def _seq_specs(nb, L, nc):
    fwd = lambda w_, i: pl.BlockSpec((nb, L, w_), lambda b, c: (b, c, i))
    bwd = lambda w_, i: pl.BlockSpec((nb, L, w_), lambda b, c: (b, nc - 1 - c, i))
    return fwd, bwd


def _chunk_specs(nb, rows, L, nc):
    fwd = pl.BlockSpec((nb, 1, rows, L), lambda b, c: (b, c, 0, 0))
    bwd = pl.BlockSpec((nb, 1, rows, L), lambda b, c: (b, nc - 1 - c, 0, 0))
    return fwd, bwd


def _round_robin(chains):
    chains = list(chains)
    while chains:
        alive = []
        for ch in chains:
            try:
                next(ch)
                alive.append(ch)
            except StopIteration:
                pass
        chains = alive


def _gla_tile(q, k, vj, e, mask_ref, d, st_ref, o_ref, first, same_head):
    L = GLA_CHUNK
    cum, e_end, c_end = e[0:L], e[L:2 * L], e[GLA_U_ROWS - SUBLANES:GLA_U_ROWS - SUBLANES + 1]
    q = q * (GLA_DK ** -0.5)
    vj_t = vj.T
    q_in = q * jnp.exp(cum)
    k_end = k * jnp.exp(e_end)
    st = st_ref[...]
    o_inter = _dot_nt(q_in, st)
    att = jnp.zeros((2 * L, L), F32)
    for lvl in range(7):
        x = None if lvl == 0 else jnp.exp(e[(1 + lvl) * L:(2 + lvl) * L])
        ql = q if lvl == 0 else q * x
        kl = k if lvl == 0 else k * x
        lhs = jnp.concatenate([jnp.where(first, ql, 0.0), jnp.where(first, 0.0, ql)], axis=0)
        att = att + _dot_nt(lhs, kl) * mask_ref[d, lvl]
        yield
    o_a = _dot(att[:L], vj[:, :GLA_DV])
    o_b = _dot(att[L:], vj[:, GLA_DV:])
    upd = _dot(vj_t, k_end)
    yield
    o_ref[...] = jnp.concatenate([o_a, o_b], axis=1) + o_inter
    st_ref[...] = st * jnp.exp(c_end) + jnp.where(same_head, upd, 0.0)
    yield


def _gla_kernel(qf, kf, vf, lrf, qb, kb, vb, lrb, a2_ref, ab_ref, u_ref, mask_ref, s0_ref,
                of_ref, ob_ref, s_out_ref, s_scr, *, n_chunks, nb):
    c = pl.program_id(1)

    @pl.when(c == 0)
    def _():
        s_scr[...] = s0_ref[...]

    first = lax.broadcasted_iota(jnp.int32, (GLA_CHUNK, LANES), 1) < GLA_DK
    row = lax.broadcasted_iota(jnp.int32, (2 * GLA_DV, LANES), 0)
    lane = lax.broadcasted_iota(jnp.int32, (2 * GLA_DV, LANES), 1)
    same_head = (row < GLA_DV) == (lane < GLA_DK)
    sides = ((qf, kf, vf, lrf, of_ref), (qb, kb, vb, lrb, ob_ref))
    samples = [(d, bi) for d in range(2) for bi in range(nb)]
    las = [_log_sigmoid(_dot(sides[d][3][bi], a2_ref[d]) + ab_ref[d]) * (1.0 / GLA_TAU)
           for d, bi in samples]
    es = []
    for (d, bi), la in zip(samples, las):
        hi, lo = _hi_lo(la)
        e2 = jnp.dot(u_ref[d], jnp.concatenate([hi, lo], axis=1), preferred_element_type=F32)
        es.append(e2[:, :GLA_QK] + e2[:, GLA_QK:])
    chains = []
    for (d, bi), e in zip(samples, es):
        q_ref, k_ref, v_ref, _, o_ref = sides[d]
        for j in range(2):
            sl = slice(j * LANES, (j + 1) * LANES)
            vsl = slice(2 * j * GLA_DV, (2 * j + 2) * GLA_DV)
            chains.append(_gla_tile(q_ref[bi, :, sl], k_ref[bi, :, sl], v_ref[bi, :, vsl], e[:, sl], mask_ref,
                                    d, s_scr.at[bi, d, j], o_ref.at[bi, :, vsl], first, same_head))
    _round_robin(chains)

    @pl.when(c == n_chunks - 1)
    def _():
        s_out_ref[...] = s_scr[...]


def _gla_scan(proj, a2p, ab, s0):
    bsz, t, _ = proj.shape
    L, nb = GLA_CHUNK, GLA_NB
    nc = t // L
    u, mask2 = _gla_constants()
    fwd, bwd = _seq_specs(nb, L, nc)
    seqs = lambda s: [s(GLA_QK, 0), s(GLA_QK, 1), s(GLA_V, 1), s(LANES, 28)]
    return pl.pallas_call(
        functools.partial(_gla_kernel, n_chunks=nc, nb=nb),
        grid=(bsz // nb, nc),
        in_specs=seqs(fwd) + seqs(bwd) + [_const_spec(a2p), _const_spec(ab), _const_spec(u),
                                          _const_spec(mask2), _state_spec(s0, nb)],
        out_specs=[fwd(GLA_V, 0), bwd(GLA_V, 0), _state_spec(s0, nb)],
        out_shape=[jax.ShapeDtypeStruct((bsz, t, GLA_V), F32)] * 2 + [jax.ShapeDtypeStruct(s0.shape, F32)],
        scratch_shapes=[pltpu.VMEM((nb,) + s0.shape[1:], F32)],
        compiler_params=_params("parallel", "arbitrary"),
        name="gla_scan",
    )(*([proj] * 8), a2p, ab, u, mask2, s0)


def _mlstm_rows(gr, gbr, trit_b):
    gates_r = gr + gbr
    hi, lo = _hi_lo(_log_sigmoid(gates_r))
    b2 = jnp.dot(jnp.concatenate([hi, lo], axis=0), trit_b, preferred_element_type=F32)
    return gates_r, b2[:16] + b2[16:]


def _running_max(g_r, reverse):
    L = g_r.shape[1]
    x = jnp.concatenate([g_r, jnp.full((1, LANES - L), NEG_INF, F32)], axis=1)
    sh = 1
    while sh < L:
        x = jnp.maximum(x, pltpu.roll(x, LANES - sh if reverse else sh, 1))
        sh *= 2
    return x[:, :L]


def _diag_hi_lo(row, eye2):
    hi, lo = _hi_lo(row)
    return jnp.where(eye2, jnp.concatenate([hi, lo], axis=1).astype(F32), 0.0)


def _mlstm_head(q, k, v, rows, tri, eye2, ones_b, d, h, cn_ref, m_ref, o_ref):
    L = ML_CHUNK
    gates_r, b_rows = rows
    end = 0 if d else L - 1
    ci_, cf_ = 8 * d + h, 8 * d + 4 + h
    b_r, li_r = b_rows[cf_:cf_ + 1, :], gates_r[ci_:ci_ + 1, :]
    b_end = b_r[:, end:end + 1]
    m = m_ref[h][:, 0:1]
    cn = cn_ref[h]
    qk = _dot_nt(q, k) * (ML_D ** -0.5)
    q_cn = _dot(q, cn)
    g_r = li_r - b_r
    mt_r = b_r + jnp.maximum(m, _running_max(g_r, bool(d)))
    d1_r = b_r - mt_r
    lhs = jnp.concatenate([_diag_hi_lo(d1_r, eye2), _diag_hi_lo(mt_r, eye2)], axis=0)
    m_new = jnp.maximum(b_end + m, jnp.max(b_end + g_r, axis=1, keepdims=True))
    decay = jnp.exp(b_end + m - m_new)
    kw_t = k.astype(F32).T * (jnp.exp(b_end + g_r - m_new) * (ML_D ** -0.5))
    v1 = jnp.concatenate([v.astype(BF16), ones_b[:L]], axis=1)
    yield
    cols = jnp.dot(lhs.astype(BF16), ones_b, preferred_element_type=F32)
    yield
    d1_c, mt_c = cols[:L], cols[L:]
    sc = qk * jnp.exp(jnp.where(tri, d1_c[:, :L] + g_r, NEG_INF))
    s_v = jnp.dot(sc.astype(BF16), v1, preferred_element_type=F32)
    yield
    aw = jnp.exp(d1_c + m)
    num = aw * q_cn[:, :ML_D] + s_v[:, :ML_D]
    den = aw * q_cn[:, ML_D:] + s_v[:, ML_D:]
    o_ref[...] = num / jnp.maximum(jnp.abs(den), jnp.exp(-mt_c))
    cn_ref[h] = decay * cn + jnp.dot(kw_t.astype(BF16), v1, preferred_element_type=F32)
    m_ref[h] = jnp.broadcast_to(m_new, (1, LANES))
    yield


def _mlstm_kernel(qf, kf, vf, grf, qb, kb, vb, grb, gbr_ref, tri_ref, trit_ref, cn0_ref, m0_ref,
                  of_ref, ob_ref, cn_out_ref, m_out_ref, cn_scr, m_scr, *, n_chunks, nb):
    c = pl.program_id(1)
    L = ML_CHUNK

    @pl.when(c == 0)
    def _():
        cn_scr[...] = cn0_ref[...]
        m_scr[...] = m0_ref[...]

    ones_b = jnp.ones((2 * L, LANES), BF16)
    eye2 = (lax.broadcasted_iota(jnp.int32, (L, 2 * L), 1) % L) == lax.broadcasted_iota(jnp.int32, (L, 2 * L), 0)
    chains = []
    for d, (q_ref, k_ref, v_ref, gr_ref, o_ref) in enumerate(((qf, kf, vf, grf, of_ref),
                                                              (qb, kb, vb, grb, ob_ref))):
        tri = tri_ref[d] > 0.5
        for bi in range(nb):
            rows = _mlstm_rows(gr_ref[bi, 0], gbr_ref[...], trit_ref[d])
            for h in range(ML_HEADS):
                sl = slice(h * ML_D, (h + 1) * ML_D)
                chains.append(_mlstm_head(q_ref[bi, :, sl], k_ref[bi, :, sl], v_ref[bi, :, sl], rows, tri, eye2,
                                          ones_b, d, h, cn_scr.at[bi, d], m_scr.at[bi, d], o_ref.at[bi, :, sl]))
    _round_robin(chains)

    @pl.when(c == n_chunks - 1)
    def _():
        cn_out_ref[...] = cn_scr[...]
        m_out_ref[...] = m_scr[...]


def _mlstm_scan(qk, proj, gates_r, gate_b, state):
    bsz, t, _ = proj.shape
    L, nb = ML_CHUNK, ML_NB
    nc = t // L
    tri_b, trit_b = _tri_pair(L)
    gbr = gate_b.reshape(16, 1)
    fwd, bwd = _seq_specs(nb, L, nc)
    gfwd, gbwd = _chunk_specs(nb, 16, L, nc)
    seqs = lambda s, g: [s(ML_W, 0), s(ML_W, 1), s(ML_W, 5), g]
    st_specs = [_state_spec(a, nb) for a in state]
    outs = pl.pallas_call(
        functools.partial(_mlstm_kernel, n_chunks=nc, nb=nb),
        grid=(bsz // nb, nc),
        in_specs=seqs(fwd, gfwd) + seqs(bwd, gbwd) + [_const_spec(gbr), _const_spec(tri_b),
                                                      _const_spec(trit_b)] + st_specs,
        out_specs=[fwd(ML_W, 0), bwd(ML_W, 0)] + st_specs,
        out_shape=[jax.ShapeDtypeStruct((bsz, t, ML_W), F32)] * 2
                  + [jax.ShapeDtypeStruct(a.shape, F32) for a in state],
        scratch_shapes=[pltpu.VMEM((nb,) + a.shape[1:], F32) for a in state],
        compiler_params=_params("parallel", "arbitrary"),
        name="mlstm_scan",
    )(*([qk, qk, proj, gates_r] * 2), gbr, tri_b, trit_b, *state)
    return outs[0], outs[1], tuple(outs[2:])


def _ssd_steps(dtc, dtr, bias_c, bias_r, a_c, a_r, tri_b, trit_b):
    dt_c = _softplus(dtc + bias_c)
    dt_r = _softplus(dtr + bias_r)
    hi, lo = _hi_lo(dt_c * (-LOG2_E * jnp.exp(a_c)))
    c2 = jnp.dot(tri_b, jnp.concatenate([hi, lo], axis=1), preferred_element_type=F32)
    cum_c = c2[:, :LANES] + c2[:, LANES:]
    hi, lo = _hi_lo(dt_r * (-LOG2_E * jnp.exp(a_r)))
    c2 = jnp.dot(jnp.concatenate([hi, lo], axis=0), trit_b, preferred_element_type=F32)
    cum_r = c2[:SSD_HEADS * 2] + c2[SSD_HEADS * 2:]
    return dt_r, cum_c, cum_r


def _ssd_group(x_ref, bg, cg, steps, tri, d, g, s_ref, y_ref, first, first_b):
    L = SSD_CHUNK
    dt_r, cum_c, cum_r = steps
    end = 0 if d else L - 1
    bg_t = bg.astype(F32).T
    if y_ref is not None:
        cb = _dot_nt(cg, bg)
        y_inter = _dot(cg, s_ref[...])
        yield
    for j in range(SSD_HPG // 2):
        ha = d * SSD_HEADS + g * SSD_HPG + 2 * j
        tile = slice(j * LANES, (j + 1) * LANES)
        xt = x_ref[:, tile]
        x2 = jnp.concatenate([xt * first_b, xt * (1 - first_b)], axis=0)
        ws, es, lhs_s, decs = [], [], [], []
        for hh in (ha, ha + 1):
            cum_row, dt_row = cum_r[hh:hh + 1, :], dt_r[hh:hh + 1, :]
            c_end = cum_row[:, end:end + 1]
            lhs_s.append(bg_t * (jnp.exp2(c_end - cum_row) * dt_row))
            decs.append(jnp.exp2(c_end))
            if y_ref is not None:
                cum_col = jnp.broadcast_to(cum_c[:, hh:hh + 1], (L, LANES))
                ws.append(jnp.exp2(jnp.where(tri, cum_col - cum_row, NEG_INF)) * cb * dt_row)
                es.append(jnp.exp2(cum_col))
        if y_ref is not None:
            y_ref[:, tile] = (_dot(jnp.concatenate(ws, axis=1), x2)
                              + jnp.where(first, es[0], es[1]) * y_inter[:, tile]).astype(y_ref.dtype)
        dec = jnp.where(first[0:1], decs[0], decs[1])
        s_ref[:, tile] = dec * s_ref[:, tile] + _dot(jnp.concatenate(lhs_s, axis=1), x2)
        yield


def _ssd_kernel(xf, bmf, cmf, dtcf, dtrf, xb, bmb, cmb, dtcb, dtrb, bias_c_ref, bias_r_ref,
                a_c_ref, a_r_ref, tri_ref, trit_ref, s0_ref, *rest, n_chunks, nb, with_out):
    yf_ref, yb_ref = (rest[0], rest[1]) if with_out else (None, None)
    s_out_ref, s_scr = rest[-2:]
    c = pl.program_id(1)

    @pl.when(c == 0)
    def _():
        s_scr[...] = s0_ref[...]

    first = lax.broadcasted_iota(jnp.int32, (SSD_CHUNK, LANES), 1) < SSD_P
    first_b = jnp.where(first, 1.0, 0.0).astype(BF16)
    chains = []
    for d, (x_ref, bm_ref, cm_ref, dtc_ref, dtr_ref, y_ref) in enumerate(
            ((xf, bmf, cmf, dtcf, dtrf, yf_ref), (xb, bmb, cmb, dtcb, dtrb, yb_ref))):
        tri = tri_ref[d] > 0.5
        for bi in range(nb):
            steps = _ssd_steps(dtc_ref[bi], dtr_ref[bi, 0], bias_c_ref[...], bias_r_ref[...], a_c_ref[...],
                               a_r_ref[...], tri_ref[d], trit_ref[d])
            for g in range(SSD_GROUPS):
                nsl = slice(g * SSD_STATE, (g + 1) * SSD_STATE)
                gsl = slice(g * SSD_HPG * SSD_P, (g + 1) * SSD_HPG * SSD_P)
                chains.append(_ssd_group(x_ref.at[bi, :, gsl], bm_ref[bi, :, nsl], cm_ref[bi, :, nsl], steps,
                                         tri, d, g, s_scr.at[bi, d, g],
                                         None if y_ref is None else y_ref.at[bi, :, gsl], first, first_b))
    _round_robin(chains)

    @pl.when(c == n_chunks - 1)
    def _():
        s_out_ref[...] = s_scr[...]


def _ssd_scan(xbc, proj, dt_r, dt_bias, a_log, s0, with_out):
    bsz, t, _ = xbc.shape
    L, nb = SSD_CHUNK, SSD_NB
    nc = t // L
    tri_b, trit_b = _tri_pair(L)
    nh = 2 * SSD_HEADS
    bias_c = jnp.zeros((1, LANES), F32).at[0, :nh].set(dt_bias.reshape(nh))
    a_c = jnp.zeros((1, LANES), F32).at[0, :nh].set(a_log.reshape(nh))
    bias_r, a_r = dt_bias.reshape(nh, 1), a_log.reshape(nh, 1)
    fwd, bwd = _seq_specs(nb, L, nc)
    gfwd, gbwd = _chunk_specs(nb, nh, L, nc)
    nbc = D_INNER // SSD_BC
    seqs = lambda s, g: [s(D_INNER, 0), s(SSD_BC, nbc), s(SSD_BC, nbc + 1), s(LANES, ODD_IN // LANES), g]
    consts = [bias_c, bias_r, a_c, a_r, tri_b, trit_b]
    out_specs, out_shape = [_state_spec(s0, nb)], [jax.ShapeDtypeStruct(s0.shape, F32)]
    if with_out:
        out_specs = [fwd(D_INNER, 0), bwd(D_INNER, 0)] + out_specs
        out_shape = [jax.ShapeDtypeStruct((bsz, t, D_INNER), BF16)] * 2 + out_shape
    outs = pl.pallas_call(
        functools.partial(_ssd_kernel, n_chunks=nc, nb=nb, with_out=with_out),
        grid=(bsz // nb, nc),
        in_specs=seqs(fwd, gfwd) + seqs(bwd, gbwd) + [_const_spec(a) for a in consts] + [_state_spec(s0, nb)],
        out_specs=out_specs, out_shape=out_shape,
        scratch_shapes=[pltpu.VMEM((nb,) + s0.shape[1:], F32)],
        compiler_params=_params("parallel", "arbitrary"),
        name="ssd_scan",
    )(*([xbc, xbc, xbc, proj, dt_r] * 2), *consts, s0)
    return (outs[0], outs[1], outs[2]) if with_out else (None, None, outs[0])


def _head_rms(y, g, width):
    parts = []
    for s in range(0, y.shape[1], width):
        p = y[:, s:s + width]
        parts.append(p * lax.rsqrt(jnp.mean(p * p, axis=-1, keepdims=True) + EPS))
    return jnp.concatenate(parts, axis=1) * g


def _even_out_kernel(ogf_ref, ogb_ref, hmf_ref, hmb_ref, gg_ref, mo_ref, gn_ref, mn_ref, w_ref,
                     x_ref, gate_ref, o_ref):
    gla = _head_rms(ogf_ref[0] + ogb_ref[0], gn_ref[...], GLA_DV) * _silu(gg_ref[0])
    ml = _head_rms(_sigmoid(mo_ref[0]) * (hmf_ref[0] + hmb_ref[0]), mn_ref[...], ML_D)
    y = _dot(gla, w_ref[:GLA_V]) + _dot(ml, w_ref[GLA_V:])
    o_ref[0] = x_ref[0] + gate_ref[0] * y


def _even_out(o_gla, h_ml, proj, gla_norm_g, ml_norm_g, w_out, x, gate, tm):
    bsz, t, d = x.shape
    tok = lambda w_, i: pl.BlockSpec((1, tm, w_), lambda b, s: (b, s, i))
    const = lambda a: pl.BlockSpec(a.shape, lambda b, s: (0,) * a.ndim)
    gn, mn = gla_norm_g.reshape(1, GLA_V), ml_norm_g.reshape(1, ML_W)
    return pl.pallas_call(
        _even_out_kernel,
        grid=(bsz, t // tm),
        in_specs=[tok(GLA_V, 0)] * 2 + [tok(ML_W, 0)] * 2 + [tok(GLA_V, 2), tok(ML_W, 6), const(gn),
                                                             const(mn), const(w_out), tok(d, 0),
                                                             _bvec_spec(gate)],
        out_specs=tok(d, 0),
        out_shape=jax.ShapeDtypeStruct(x.shape, F32),
        compiler_params=_params("parallel", "parallel"),
        name="even_out",
    )(*o_gla, *h_ml, proj, proj, gn, mn, w_out, x, gate)


def _odd_out_kernel(yf_ref, yb_ref, xs_ref, z_ref, dsk_ref, ng_ref, w_ref, x_ref, gate_ref, o_ref):
    y = yf_ref[0].astype(F32) + yb_ref[0].astype(F32) + dsk_ref[...] * xs_ref[0].astype(F32)
    y = y * _silu(z_ref[0])
    y = _head_rms(y, ng_ref[...], D_INNER // SSD_GROUPS)
    o_ref[0] = x_ref[0] + gate_ref[0] * _dot(y, w_ref[...])


def _odd_out(y, xbc, proj, d_skip, norm_g, w_out, x, gate, tm):
    bsz, t, d = x.shape
    tok = lambda w_, i: pl.BlockSpec((1, tm, w_), lambda b, s: (b, s, i))
    const = lambda a: pl.BlockSpec(a.shape, lambda b, s: (0,) * a.ndim)
    dsk = jnp.repeat(d_skip, SSD_P).reshape(1, D_INNER)
    ng = norm_g.reshape(1, D_INNER)
    return pl.pallas_call(
        _odd_out_kernel,
        grid=(bsz, t // tm),
        in_specs=[tok(D_INNER, 0)] * 4 + [const(dsk), const(ng), const(w_out), tok(d, 0), _bvec_spec(gate)],
        out_specs=tok(d, 0),
        out_shape=jax.ShapeDtypeStruct(x.shape, F32),
        compiler_params=_params("parallel", "parallel"),
        name="odd_out",
    )(*y, xbc, proj, dsk, ng, w_out, x, gate)


FFN_TB = 256
FFN_PAD = GRID_W
FFN_CT = 256


def _ffn_shift_matrix(grid_conv):
    i = np.arange(FFN_TB)[:, None]
    j = np.arange(FFN_TB)[None, :]
    prev, nxt = (j == i - 1), (j == i + 1)
    if grid_conv:
        prev &= (i % GRID_W != 0)
        nxt &= (i % GRID_W != GRID_W - 1)
    return jnp.asarray(np.concatenate([prev, nxt], axis=0), BF16)


def _ffn_conv_block(uc_scr, ul_scr, ur_scr, r0, w9, b, grid_conv):
    acc = b.astype(BF16)
    for dr in ((-1, 0, 1) if grid_conv else (0,)):
        rows = pl.ds(FFN_PAD + r0 + GRID_W * dr, FFN_TB)
        i = 3 * (dr + 1)
        acc = acc + (ul_scr[rows, :] * w9[i:i + 1].astype(BF16) + uc_scr[rows, :] * w9[i + 1:i + 2].astype(BF16)
                     + ur_scr[rows, :] * w9[i + 2:i + 3].astype(BF16))
    return acc


def _ffn_up_kernel(x_ref, g_ref, sc_ref, sh_ref, wa_ref, wg_ref, cwa_ref, cwg_ref, cba_ref, cbg_ref, shift_ref,
                   o_ref, h_scr, *u_scrs, grid_conv):
    t = x_ref.shape[1]
    ct = o_ref.shape[2]

    @pl.when(pl.program_id(1) == 0)
    def _():
        h_scr[...] = _norm_mod(x_ref[0], g_ref[...], sc_ref[0], sh_ref[0]).astype(BF16)
        zeros = jnp.zeros((FFN_PAD, ct), BF16)
        for scr in u_scrs:
            scr[pl.ds(0, FFN_PAD), :] = zeros
            scr[pl.ds(FFN_PAD + t, FFN_PAD), :] = zeros

    halves = ((wa_ref, cwa_ref, cba_ref, u_scrs[0:3]), (wg_ref, cwg_ref, cbg_ref, u_scrs[3:6]))

    def up_block(blk):
        hb = h_scr[pl.ds(blk * FFN_TB, FFN_TB), :]
        for w_ref, _, _, (uc_scr, _, _) in halves:
            uc_scr[pl.ds(FFN_PAD + blk * FFN_TB, FFN_TB), :] = jnp.dot(
                hb, w_ref[...], preferred_element_type=F32).astype(BF16)

    def shift_block(blk):
        rows = pl.ds(FFN_PAD + blk * FFN_TB, FFN_TB)
        for _, _, _, (uc_scr, ul_scr, ur_scr) in halves:
            lr = jnp.dot(shift_ref[...], uc_scr[rows, :], preferred_element_type=F32).astype(BF16)
            ul_scr[rows, :] = lr[:FFN_TB]
            ur_scr[rows, :] = lr[FFN_TB:]

    def conv_block(blk):
        r0 = blk * FFN_TB
        a, g = [_ffn_conv_block(*scrs, r0, cw_ref, cb_ref[...], grid_conv).astype(F32)
                for _, cw_ref, cb_ref, scrs in halves]
        o_ref[0, pl.ds(r0, FFN_TB), :] = (_silu(g) * a).astype(BF16)

    nblk = t // FFN_TB
    for step in range(nblk + 3):
        if step < nblk:
            up_block(step)
        if 0 <= step - 1 < nblk:
            shift_block(step - 1)
        if 0 <= step - 3 < nblk:
            conv_block(step - 3)


def _ffn_up(x, g, sc, sh, w_up, conv_w9, conv_b, grid_conv):
    bsz, t, d = x.shape
    ct = FFN_CT
    nct = D_FF // ct
    cb = conv_b.reshape(1, 2 * D_FF)
    assert t % FFN_TB == 0 and (grid_conv or t == FFN_TB)
    shift = _ffn_shift_matrix(grid_conv)
    return pl.pallas_call(
        functools.partial(_ffn_up_kernel, grid_conv=grid_conv),
        grid=(bsz, nct),
        in_specs=[pl.BlockSpec((1, t, d), lambda b, j: (b, 0, 0)),
                  pl.BlockSpec((1, d), lambda b, j: (0, 0)),
                  _bvec_spec(sc), _bvec_spec(sh),
                  pl.BlockSpec((d, ct), lambda b, j: (0, j)),
                  pl.BlockSpec((d, ct), lambda b, j: (0, nct + j)),
                  pl.BlockSpec((9, ct), lambda b, j: (0, j)),
                  pl.BlockSpec((9, ct), lambda b, j: (0, nct + j)),
                  pl.BlockSpec((1, ct), lambda b, j: (0, j)),
                  pl.BlockSpec((1, ct), lambda b, j: (0, nct + j)),
                  pl.BlockSpec(shift.shape, lambda b, j: (0, 0))],
        out_specs=pl.BlockSpec((1, t, ct), lambda b, j: (b, 0, j)),
        out_shape=jax.ShapeDtypeStruct((bsz, t, D_FF), BF16),
        scratch_shapes=[pltpu.VMEM((t, d), BF16)] + [pltpu.VMEM((t + 2 * FFN_PAD, ct), BF16)] * 6,
        compiler_params=_params("parallel", "arbitrary"),
        name="ffn_up",
    )(x, g.reshape(1, d), sc, sh, w_up, w_up, conv_w9, conv_w9, cb, cb, shift)


def _ffn_down_kernel(a_ref, w_ref, x_ref, gate_ref, fg_ref, o_ref, *, final_norm):
    y = x_ref[0] + gate_ref[0] * jnp.dot(a_ref[0], w_ref[...], preferred_element_type=F32)
    if final_norm:
        y = y * lax.rsqrt(jnp.mean(y * y, axis=-1, keepdims=True) + EPS) * fg_ref[...]
    o_ref[0] = y


def _ffn_down(act, w_down, x, gate, final_g, final_norm, tm):
    bsz, t, d = x.shape
    tok = lambda w_: pl.BlockSpec((1, tm, w_), lambda b, s: (b, s, 0))
    const = lambda a: pl.BlockSpec(a.shape, lambda b, s: (0,) * a.ndim)
    fg = final_g.reshape(1, d)
    return pl.pallas_call(
        functools.partial(_ffn_down_kernel, final_norm=final_norm),
        grid=(bsz, t // tm),
        in_specs=[tok(D_FF), const(w_down), tok(d), _bvec_spec(gate), const(fg)],
        out_specs=tok(d),
        out_shape=jax.ShapeDtypeStruct(x.shape, F32),
        compiler_params=_params("parallel", "parallel"),
        name="ffn_down",
    )(act, w_down, x, gate, fg)


def _rows_to_chunks(a, L):
    bsz, t, ch = a.shape
    return a.reshape(bsz, t // L, L, ch).transpose(0, 1, 3, 2)


def _even_mixer(xc, xl, mods_c, mods_l, norm_g, p):
    (sh1c, sc1c, g1c), (sh1l, sc1l, g1l) = mods_c, mods_l
    w_in = jnp.concatenate([p["w_in"], p["a1"][0], p["a1"][1],
                            jnp.zeros((D_MODEL, EVEN_N - EVEN_IN - 2 * GLA_RANK), F32)], axis=1).astype(BF16)
    mq_off = GLA_QK * 2 + GLA_V * 2
    projs, qks = {}, {}
    for s, (xs, sc1, sh1) in (("c", (xc, sc1c, sh1c)), ("l", (xl, sc1l, sh1l))):
        projs[s], qks[s] = _norm_proj(xs, norm_g, sc1, sh1, w_in, 256, mq_off, p["conv_w"], p["conv_b"])
    gates_r = {s: _rows_to_chunks(projs[s][:, :, EVEN_IN - 16:EVEN_IN], ML_CHUNK) for s in projs}
    bsz = xl.shape[0]

    a2p = jnp.stack([jnp.zeros((LANES, GLA_QK), F32).at[16 + 16 * d:32 + 16 * d].set(p["a2"][d])
                     for d in range(2)]).astype(BF16)
    ab = p["ab"].reshape(2, 1, GLA_QK)
    gla_state = jnp.zeros((bsz, 2, 2, 2 * GLA_DV, LANES), F32)
    ml_state = (jnp.zeros((bsz, 2, ML_HEADS, ML_D, 2 * ML_D), F32), jnp.zeros((bsz, 2, ML_HEADS, 1, LANES), F32))
    o_gla, h_ml = {}, {}
    for s in ("c", "l"):
        of, ob, gla_state = _gla_scan(projs[s], a2p, ab, gla_state)
        o_gla[s] = (of, ob)
        hf, hb, ml_state = _mlstm_scan(qks[s], projs[s], gates_r[s], p["gate_b"], ml_state)
        h_ml[s] = (hf, hb)

    w_out = p["w_out"].astype(BF16)
    xc = _even_out(o_gla["c"], h_ml["c"], projs["c"], p["gla_norm_g"], p["ml_norm_g"], w_out, xc, g1c, 256)
    xl = _even_out(o_gla["l"], h_ml["l"], projs["l"], p["gla_norm_g"], p["ml_norm_g"], w_out, xl, g1l, 512)
    return xc, xl


def _odd_mixer_last(xc, xl, mods_c, mods_l, norm_g, p):
    (sh1c, sc1c, _), (sh1l, sc1l, g1l) = mods_c, mods_l
    w_in = jnp.concatenate([p["w_in"], jnp.zeros((D_MODEL, ODD_N - ODD_IN), F32)], axis=1).astype(BF16)
    projs, xbcs = {}, {}
    for s, (xs, sc1, sh1) in (("c", (xc, sc1c, sh1c)), ("l", (xl, sc1l, sh1l))):
        projs[s], xbcs[s] = _norm_proj(xs, norm_g, sc1, sh1, w_in, 256, D_INNER, p["conv_w"], p["conv_b"])
    dt_rs = {s: _rows_to_chunks(projs[s][:, :, ODD_IN - 2 * SSD_HEADS:ODD_IN], SSD_CHUNK) for s in projs}
    bsz = xl.shape[0]
    state = jnp.zeros((bsz, 2, SSD_GROUPS, SSD_STATE, SSD_HPG * SSD_P), F32)
    _, _, state = _ssd_scan(xbcs["c"], projs["c"], dt_rs["c"], p["dt_bias"], p["a_log"], state, False)
    yf, yb, _ = _ssd_scan(xbcs["l"], projs["l"], dt_rs["l"], p["dt_bias"], p["a_log"], state, True)
    return _odd_out((yf, yb), xbcs["l"], projs["l"], p["d_skip"], p["norm_g"], p["w_out"].astype(BF16),
                    xl, g1l, 256)


def kernel(x, c, ctx, c_ctx, mod_w, mod_b, norm_mix_g, norm_ffn_g, final_norm_g, ffn_w_up, ffn_conv_w, ffn_conv_b, ffn_w_down, even_w_in, gla_a1, gla_a2, gla_ab, ml_conv_w, ml_conv_b, ml_gate_b, gla_norm_g, ml_norm_g, even_w_out, ssd_w_in, ssd_conv_w, ssd_conv_b, ssd_dt_bias, ssd_a_log, ssd_d, ssd_norm_g, ssd_w_out):
    depth = mod_w.shape[0]
    bsz = x.shape[0]
    assert depth == 2 and x.shape[1] % (GRID_W * SUBLANES) == 0
    assert bsz % GLA_NB == 0 and bsz % ML_NB == 0 and bsz % SSD_NB == 0
    rows = -(-(bsz + 1) // SUBLANES) * SUBLANES
    cc = jnp.zeros((rows, D_MODEL), F32).at[:bsz].set(c).at[bsz].set(c_ctx)
    mod = _modulation(cc, mod_w, mod_b)

    def mods(layer, lo, hi):
        return [mod[layer, lo:hi, i * D_MODEL:(i + 1) * D_MODEL][:, None, :] for i in range(6)]

    xl, xc = x, ctx
    for layer in range(depth):
        last = layer == depth - 1
        sh1l, sc1l, g1l, sh2l, sc2l, g2l = mods(layer, 0, bsz)
        sh1c, sc1c, g1c, sh2c, sc2c, g2c = mods(layer, bsz, bsz + 1)
        if layer == 0:
            p = dict(w_in=even_w_in[0], a1=gla_a1[0], a2=gla_a2[0], ab=gla_ab[0], conv_w=ml_conv_w[0],
                     conv_b=ml_conv_b[0], gate_b=ml_gate_b[0], gla_norm_g=gla_norm_g[0],
                     ml_norm_g=ml_norm_g[0], w_out=even_w_out[0])
            xc, xl = _even_mixer(xc, xl, (sh1c, sc1c, g1c), (sh1l, sc1l, g1l), norm_mix_g[layer], p)
        else:
            p = dict(w_in=ssd_w_in[0], conv_w=ssd_conv_w[0], conv_b=ssd_conv_b[0], dt_bias=ssd_dt_bias[0],
                     a_log=ssd_a_log[0], d_skip=ssd_d[0], norm_g=ssd_norm_g[0], w_out=ssd_w_out[0])
            xl = _odd_mixer_last(xc, xl, (sh1c, sc1c, g1c), (sh1l, sc1l, g1l), norm_mix_g[layer], p)
        w_up = ffn_w_up[layer].astype(BF16)
        w_down = ffn_w_down[layer].astype(BF16)
        cw9 = ffn_conv_w[layer].reshape(9, 2 * D_FF)
        act = _ffn_up(xl, norm_ffn_g[layer], sc2l, sh2l, w_up, cw9, ffn_conv_b[layer], True)
        xl = _ffn_down(act, w_down, xl, g2l, final_norm_g, last, 512)
        if not last:
            act = _ffn_up(xc, norm_ffn_g[layer], sc2c, sh2c, w_up, cw9, ffn_conv_b[layer], False)
            xc = _ffn_down(act, w_down, xc, g2c, final_norm_g, False, 256)
    return xl
```

```python
import functools

import numpy as np
import jax
import jax.numpy as jnp
from jax import lax
from jax.experimental import pallas as pl
from jax.experimental.pallas import tpu as pltpu

F32 = jnp.float32
BF16 = jnp.bfloat16

D_MODEL = 1024
GRID_W = 64
EPS = 1e-6

GLA_HEADS, GLA_DK, GLA_DV, GLA_RANK, GLA_TAU, GLA_CHUNK = 4, 64, 128, 16, 16.0, 64
ML_HEADS, ML_D, ML_CHUNK = 4, 128, 64
D_INNER, SSD_HEADS, SSD_GROUPS, SSD_HPG, SSD_P, SSD_STATE, SSD_CHUNK = 2048, 32, 4, 8, 64, 128, 128
D_FF = 2816
GLA_QK, GLA_V, ML_W = GLA_HEADS * GLA_DK, GLA_HEADS * GLA_DV, ML_HEADS * ML_D
EVEN_IN = 3600
ODD_IN = 5184
SSD_BC = SSD_GROUPS * SSD_STATE
SSD_CONV_CH = D_INNER + 2 * SSD_BC

LANES = 128
SUBLANES = 8
EVEN_N = 3712
ODD_N = 5248
V7X_VMEM_BYTES = 64 * 1024 * 1024
VMEM_LIMIT = V7X_VMEM_BYTES - 8 * 1024 * 1024

GLA_NB = 4
ML_NB = 4
SSD_NB = 2

NEG_INF = float("-inf")
LOG2_E = 1.4426950408889634


def _params(*sem):
    return pltpu.CompilerParams(dimension_semantics=sem, vmem_limit_bytes=VMEM_LIMIT)


def _sigmoid(x):
    return 1.0 / (1.0 + jnp.exp(-x))


def _silu(x):
    return x * _sigmoid(x)


def _softplus(x):
    return jnp.maximum(x, 0.0) + jnp.log(1.0 + jnp.exp(-jnp.abs(x)))


def _log_sigmoid(x):
    return -_softplus(-x)


def _hi_lo(x):
    hi = x.astype(BF16)
    lo = (x - hi.astype(F32)).astype(BF16)
    return hi, lo


def _dot(a, b):
    return jnp.dot(a.astype(BF16), b.astype(BF16), preferred_element_type=F32)


def _dot_nt(a, b):
    return lax.dot_general(a.astype(BF16), b.astype(BF16), (((1,), (1,)), ((), ())),
                           preferred_element_type=F32)


def _norm_mod(x, g, sc, sh):
    y = x * lax.rsqrt(jnp.mean(x * x, axis=-1, keepdims=True) + EPS)
    return (y * g) * (1.0 + sc) + sh


def _mod_kernel(c_ref, w_ref, b_ref, o_ref):
    s = _silu(c_ref[...])
    hi, lo = _hi_lo(s)
    w = w_ref[0]
    whi, wlo = _hi_lo(w)
    acc = jnp.dot(hi, whi, preferred_element_type=F32)
    acc += jnp.dot(lo, whi, preferred_element_type=F32)
    acc += jnp.dot(hi, wlo, preferred_element_type=F32)
    o_ref[0] = acc + b_ref[0]


def _modulation(cc, mod_w, mod_b):
    depth, d, n = mod_w.shape
    rows = cc.shape[0]
    tn = 1536
    return pl.pallas_call(
        _mod_kernel,
        grid=(depth, n // tn),
        in_specs=[pl.BlockSpec((rows, d), lambda l, j: (0, 0)),
                  pl.BlockSpec((1, d, tn), lambda l, j: (l, 0, j)),
                  pl.BlockSpec((1, 1, tn), lambda l, j: (l, 0, j))],
        out_specs=pl.BlockSpec((1, rows, tn), lambda l, j: (l, 0, j)),
        out_shape=jax.ShapeDtypeStruct((depth, rows, n), F32),
        compiler_params=_params("parallel", "parallel"),
        name="modulation",
    )(cc, mod_w, mod_b.reshape(depth, 1, n))


def _norm_proj_kernel(x_ref, xp_ref, xn_ref, g_ref, sc_ref, sh_ref, w_ref, cw_ref, cb_ref, o_ref, oc_ref, *,
                      conv_col0):
    i = pl.program_id(1)
    tm = x_ref.shape[1]
    norm = lambda x: _norm_mod(x, g_ref[...], sc_ref[0], sh_ref[0]).astype(BF16)
    hb = norm(x_ref[0])
    hb_halo = jnp.concatenate([hb, norm(xp_ref[0]), norm(xn_ref[0])], axis=0)
    row = lax.broadcasted_iota(jnp.int32, (tm, LANES), 0)
    first_tile, last_tile = i == 0, i == pl.num_programs(1) - 1
    n = w_ref.shape[1]
    ch = oc_ref.shape[2]
    chunk = 1024
    halos = {}

    def conv_slice(s):
        u = o_ref[0, :, conv_col0 + s:conv_col0 + s + LANES]
        uh = halos.pop(s)
        before = jnp.where(first_tile, 0.0, uh[SUBLANES - 1:SUBLANES])
        after = jnp.where(last_tile, 0.0, uh[SUBLANES:SUBLANES + 1])
        prev = jnp.where(row == 0, before, pltpu.roll(u, 1, 0))
        nxt = jnp.where(row == tm - 1, after, pltpu.roll(u, tm - 1, 0))
        cw = cw_ref[:, s:s + LANES]
        y = prev * cw[0:1] + u * cw[1:2] + nxt * cw[2:3] + cb_ref[:, s:s + LANES]
        oc_ref[0, :, s:s + LANES] = _silu(y).astype(oc_ref.dtype)

    pending = list(range(0, ch, LANES))
    for s in range(0, n, chunk):
        e = min(s + chunk, n)
        with_halo = s < conv_col0 + ch and e > conv_col0
        r = jnp.dot(hb_halo if with_halo else hb, w_ref[:, s:e], preferred_element_type=F32)
        o_ref[0, :, s:e] = r[:tm]
        for c in range(max(s, conv_col0), min(e, conv_col0 + ch), LANES):
            halos[c - conv_col0] = r[tm:, c - s:c - s + LANES]
        while pending and conv_col0 + pending[0] + LANES <= s:
            conv_slice(pending.pop(0))
    for s in pending:
        conv_slice(s)


def _bvec_spec(v):
    d = v.shape[-1]
    if v.shape[0] == 1:
        return pl.BlockSpec((1, 1, d), lambda b, t: (0, 0, 0))
    return pl.BlockSpec((1, 1, d), lambda b, t: (b, 0, 0))


def _norm_proj(x, g, sc, sh, w, tm, conv_col0, conv_w, conv_b):
    bsz, t, d = x.shape
    n = w.shape[1]
    ch = conv_w.shape[1]
    hb = tm // SUBLANES
    last = t // SUBLANES - 1
    return pl.pallas_call(
        functools.partial(_norm_proj_kernel, conv_col0=conv_col0),
        grid=(bsz, t // tm),
        in_specs=[pl.BlockSpec((1, tm, d), lambda b, i: (b, i, 0)),
                  pl.BlockSpec((1, SUBLANES, d), lambda b, i: (b, jnp.maximum(i * hb - 1, 0), 0)),
                  pl.BlockSpec((1, SUBLANES, d), lambda b, i: (b, jnp.minimum((i + 1) * hb, last), 0)),
                  pl.BlockSpec((1, d), lambda b, i: (0, 0)),
                  _bvec_spec(sc), _bvec_spec(sh),
                  pl.BlockSpec((d, n), lambda b, i: (0, 0)),
                  pl.BlockSpec((3, ch), lambda b, i: (0, 0)),
                  pl.BlockSpec((1, ch), lambda b, i: (0, 0))],
        out_specs=[pl.BlockSpec((1, tm, n), lambda b, i: (b, i, 0)),
                   pl.BlockSpec((1, tm, ch), lambda b, i: (b, i, 0))],
        out_shape=[jax.ShapeDtypeStruct((bsz, t, n), F32), jax.ShapeDtypeStruct((bsz, t, ch), BF16)],
        compiler_params=_params("parallel", "parallel"),
        name="norm_proj",
    )(x, x, x, g.reshape(1, d), sc, sh, w, conv_w, conv_b.reshape(1, ch))


def _flip2(a):
    return a[..., ::-1, ::-1].copy()


GLA_LEVELS = (32, 16, 8, 4, 2, 1)
GLA_U_ROWS = (2 + len(GLA_LEVELS)) * GLA_CHUNK + SUBLANES


def _gla_constants():
    L = GLA_CHUNK
    t = np.arange(L)[:, None]
    j = np.arange(L)[None, :]
    blocks = [(j <= t), (j > t)]
    masks = [np.eye(L)]
    for m in GLA_LEVELS:
        pos, blk = t % (2 * m), t // (2 * m)
        ref = blk * 2 * m + m - 1
        blocks.append(((pos >= m) & (j > ref) & (j <= t)) | ((pos < m) & (j > t) & (j <= ref)))
        masks.append((blk == blk.T) & (pos >= m) & (pos.T < m))
    blocks = [np.asarray(b_, np.float32) for b_ in blocks]
    masks = [np.asarray(m_, np.float32) for m_ in masks]
    us, mask2s = [], []
    for reverse in (False, True):
        bl = [_flip2(b_) for b_ in blocks] if reverse else blocks
        ms = [_flip2(m_) for m_ in masks] if reverse else masks
        us.append(np.concatenate(bl + [np.ones((SUBLANES, L), np.float32)], axis=0))
        mask2s.append(np.stack([np.concatenate([m_, m_], axis=0) for m_ in ms]))
    return jnp.asarray(np.stack(us), BF16), jnp.asarray(np.stack(mask2s), F32)


def _tri_pair(L):
    t = np.arange(L)[:, None]
    j = np.arange(L)[None, :]
    tri = np.stack([np.asarray(j <= t, np.float32), np.asarray(j >= t, np.float32)])
    return jnp.asarray(tri, BF16), jnp.asarray(tri.transpose(0, 2, 1), BF16)


def _const_spec(a):
    return pl.BlockSpec(a.shape, lambda b, c: (0,) * a.ndim)


def _state_spec(a, nb):
    return pl.BlockSpec((nb,) + a.shape[1:], lambda b, c: (b,) + (0,) * (a.ndim - 1))


def _seq_specs(nb, L, nc):
    fwd = lambda w_, i: pl.BlockSpec((nb, L, w_), lambda b, c: (b, c, i))
    bwd = lambda w_, i: pl.BlockSpec((nb, L, w_), lambda b, c: (b, nc - 1 - c, i))
    return fwd, bwd


def _chunk_specs(nb, rows, L, nc):
    fwd = pl.BlockSpec((nb, 1, rows, L), lambda b, c: (b, c, 0, 0))
    bwd = pl.BlockSpec((nb, 1, rows, L), lambda b, c: (b, nc - 1 - c, 0, 0))
    return fwd, bwd


def _round_robin(chains):
    chains = list(chains)
    while chains:
        alive = []
        for ch in chains:
            try:
                next(ch)
                alive.append(ch)
            except StopIteration:
                pass
        chains = alive


def _gla_tile(q, k, vj, e, mask_ref, d, st_ref, o_ref, first, same_head):
    L = GLA_CHUNK
    cum, e_end, c_end = e[0:L], e[L:2 * L], e[GLA_U_ROWS - SUBLANES:GLA_U_ROWS - SUBLANES + 1]
    q = q * (GLA_DK ** -0.5)
    vj_t = vj.T
    q_in = q * jnp.exp(cum)
    k_end = k * jnp.exp(e_end)
    st = st_ref[...]
    o_inter = _dot_nt(q_in, st)
    att = jnp.zeros((2 * L, L), F32)
    for lvl in range(7):
        x = None if lvl == 0 else jnp.exp(e[(1 + lvl) * L:(2 + lvl) * L])
        ql = q if lvl == 0 else q * x
        kl = k if lvl == 0 else k * x
        lhs = jnp.concatenate([jnp.where(first, ql, 0.0), jnp.where(first, 0.0, ql)], axis=0)
        att = att + _dot_nt(lhs, kl) * mask_ref[d, lvl]
        yield
    o_a = _dot(att[:L], vj[:, :GLA_DV])
    o_b = _dot(att[L:], vj[:, GLA_DV:])
    upd = _dot(vj_t, k_end)
    yield
    o_ref[...] = jnp.concatenate([o_a, o_b], axis=1) + o_inter
    st_ref[...] = st * jnp.exp(c_end) + jnp.where(same_head, upd, 0.0)
    yield


def _gla_kernel(qf, kf, vf, lrf, qb, kb, vb, lrb, a2_ref, ab_ref, u_ref, mask_ref, s0_ref,
                of_ref, ob_ref, s_out_ref, s_scr, *, n_chunks, nb):
    c = pl.program_id(1)

    @pl.when(c == 0)
    def _():
        s_scr[...] = s0_ref[...]

    first = lax.broadcasted_iota(jnp.int32, (GLA_CHUNK, LANES), 1) < GLA_DK
    row = lax.broadcasted_iota(jnp.int32, (2 * GLA_DV, LANES), 0)
    lane = lax.broadcasted_iota(jnp.int32, (2 * GLA_DV, LANES), 1)
    same_head = (row < GLA_DV) == (lane < GLA_DK)
    sides = ((qf, kf, vf, lrf, of_ref), (qb, kb, vb, lrb, ob_ref))
    samples = [(d, bi) for d in range(2) for bi in range(nb)]
    las = [_log_sigmoid(_dot(sides[d][3][bi], a2_ref[d]) + ab_ref[d]) * (1.0 / GLA_TAU)
           for d, bi in samples]
    es = []
    for (d, bi), la in zip(samples, las):
        hi, lo = _hi_lo(la)
        e2 = jnp.dot(u_ref[d], jnp.concatenate([hi, lo], axis=1), preferred_element_type=F32)
        es.append(e2[:, :GLA_QK] + e2[:, GLA_QK:])
    chains = []
    for (d, bi), e in zip(samples, es):
        q_ref, k_ref, v_ref, _, o_ref = sides[d]
        for j in range(2):
            sl = slice(j * LANES, (j + 1) * LANES)
            vsl = slice(2 * j * GLA_DV, (2 * j + 2) * GLA_DV)
            chains.append(_gla_tile(q_ref[bi, :, sl], k_ref[bi, :, sl], v_ref[bi, :, vsl], e[:, sl], mask_ref,
                                    d, s_scr.at[bi, d, j], o_ref.at[bi, :, vsl], first, same_head))
    _round_robin(chains)

    @pl.when(c == n_chunks - 1)
    def _():
        s_out_ref[...] = s_scr[...]


def _gla_scan(proj, a2p, ab, s0):
    bsz, t, _ = proj.shape
    L, nb = GLA_CHUNK, GLA_NB
    nc = t // L
    u, mask2 = _gla_constants()
    fwd, bwd = _seq_specs(nb, L, nc)
    seqs = lambda s: [s(GLA_QK, 0), s(GLA_QK, 1), s(GLA_V, 1), s(LANES, 28)]
    return pl.pallas_call(
        functools.partial(_gla_kernel, n_chunks=nc, nb=nb),
        grid=(bsz // nb, nc),
        in_specs=seqs(fwd) + seqs(bwd) + [_const_spec(a2p), _const_spec(ab), _const_spec(u),
                                          _const_spec(mask2), _state_spec(s0, nb)],
        out_specs=[fwd(GLA_V, 0), bwd(GLA_V, 0), _state_spec(s0, nb)],
        out_shape=[jax.ShapeDtypeStruct((bsz, t, GLA_V), F32)] * 2 + [jax.ShapeDtypeStruct(s0.shape, F32)],
        scratch_shapes=[pltpu.VMEM((nb,) + s0.shape[1:], F32)],
        compiler_params=_params("parallel", "arbitrary"),
        name="gla_scan",
    )(*([proj] * 8), a2p, ab, u, mask2, s0)


def _mlstm_rows(gr, gbr, trit_b):
    gates_r = gr + gbr
    hi, lo = _hi_lo(_log_sigmoid(gates_r))
    b2 = jnp.dot(jnp.concatenate([hi, lo], axis=0), trit_b, preferred_element_type=F32)
    return gates_r, b2[:16] + b2[16:]


def _running_max(g_r, reverse):
    L = g_r.shape[1]
    x = jnp.concatenate([g_r, jnp.full((1, LANES - L), NEG_INF, F32)], axis=1)
    sh = 1
    while sh < L:
        x = jnp.maximum(x, pltpu.roll(x, LANES - sh if reverse else sh, 1))
        sh *= 2
    return x[:, :L]


def _diag_hi_lo(row, eye2):
    hi, lo = _hi_lo(row)
    return jnp.where(eye2, jnp.concatenate([hi, lo], axis=1).astype(F32), 0.0)


def _mlstm_head(q, k, v, rows, tri, eye2, ones_b, d, h, cn_ref, m_ref, o_ref):
    L = ML_CHUNK
    gates_r, b_rows = rows
    end = 0 if d else L - 1
    ci_, cf_ = 8 * d + h, 8 * d + 4 + h
    b_r, li_r = b_rows[cf_:cf_ + 1, :], gates_r[ci_:ci_ + 1, :]
    b_end = b_r[:, end:end + 1]
    m = m_ref[h][:, 0:1]
    cn = cn_ref[h]
    qk = _dot_nt(q, k) * (ML_D ** -0.5)
    q_cn = _dot(q, cn)
    g_r = li_r - b_r
    mt_r = b_r + jnp.maximum(m, _running_max(g_r, bool(d)))
    d1_r = b_r - mt_r
    lhs = jnp.concatenate([_diag_hi_lo(d1_r, eye2), _diag_hi_lo(mt_r, eye2)], axis=0)
    m_new = jnp.maximum(b_end + m, jnp.max(b_end + g_r, axis=1, keepdims=True))
    decay = jnp.exp(b_end + m - m_new)
    kw_t = k.astype(F32).T * (jnp.exp(b_end + g_r - m_new) * (ML_D ** -0.5))
    v1 = jnp.concatenate([v.astype(BF16), ones_b[:L]], axis=1)
    yield
    cols = jnp.dot(lhs.astype(BF16), ones_b, preferred_element_type=F32)
    yield
    d1_c, mt_c = cols[:L], cols[L:]
    sc = qk * jnp.exp(jnp.where(tri, d1_c[:, :L] + g_r, NEG_INF))
    s_v = jnp.dot(sc.astype(BF16), v1, preferred_element_type=F32)
    yield
    aw = jnp.exp(d1_c + m)
    num = aw * q_cn[:, :ML_D] + s_v[:, :ML_D]
    den = aw * q_cn[:, ML_D:] + s_v[:, ML_D:]
    o_ref[...] = num / jnp.maximum(jnp.abs(den), jnp.exp(-mt_c))
    cn_ref[h] = decay * cn + jnp.dot(kw_t.astype(BF16), v1, preferred_element_type=F32)
    m_ref[h] = jnp.broadcast_to(m_new, (1, LANES))
    yield


def _mlstm_kernel(qf, kf, vf, grf, qb, kb, vb, grb, gbr_ref, tri_ref, trit_ref, cn0_ref, m0_ref,
                  of_ref, ob_ref, cn_out_ref, m_out_ref, cn_scr, m_scr, *, n_chunks, nb):
    c = pl.program_id(1)
    L = ML_CHUNK

    @pl.when(c == 0)
    def _():
        cn_scr[...] = cn0_ref[...]
        m_scr[...] = m0_ref[...]

    ones_b = jnp.ones((2 * L, LANES), BF16)
    eye2 = (lax.broadcasted_iota(jnp.int32, (L, 2 * L), 1) % L) == lax.broadcasted_iota(jnp.int32, (L, 2 * L), 0)
    chains = []
    for d, (q_ref, k_ref, v_ref, gr_ref, o_ref) in enumerate(((qf, kf, vf, grf, of_ref),
                                                              (qb, kb, vb, grb, ob_ref))):
        tri = tri_ref[d] > 0.5
        for bi in range(nb):
            rows = _mlstm_rows(gr_ref[bi, 0], gbr_ref[...], trit_ref[d])
            for h in range(ML_HEADS):
                sl = slice(h * ML_D, (h + 1) * ML_D)
                chains.append(_mlstm_head(q_ref[bi, :, sl], k_ref[bi, :, sl], v_ref[bi, :, sl], rows, tri, eye2,
                                          ones_b, d, h, cn_scr.at[bi, d], m_scr.at[bi, d], o_ref.at[bi, :, sl]))
    _round_robin(chains)

    @pl.when(c == n_chunks - 1)
    def _():
        cn_out_ref[...] = cn_scr[...]
        m_out_ref[...] = m_scr[...]


def _mlstm_scan(qk, proj, gates_r, gate_b, state):
    bsz, t, _ = proj.shape
    L, nb = ML_CHUNK, ML_NB
    nc = t // L
    tri_b, trit_b = _tri_pair(L)
    gbr = gate_b.reshape(16, 1)
    fwd, bwd = _seq_specs(nb, L, nc)
    gfwd, gbwd = _chunk_specs(nb, 16, L, nc)
    seqs = lambda s, g: [s(ML_W, 0), s(ML_W, 1), s(ML_W, 5), g]
    st_specs = [_state_spec(a, nb) for a in state]
    outs = pl.pallas_call(
        functools.partial(_mlstm_kernel, n_chunks=nc, nb=nb),
        grid=(bsz // nb, nc),
        in_specs=seqs(fwd, gfwd) + seqs(bwd, gbwd) + [_const_spec(gbr), _const_spec(tri_b),
                                                      _const_spec(trit_b)] + st_specs,
        out_specs=[fwd(ML_W, 0), bwd(ML_W, 0)] + st_specs,
        out_shape=[jax.ShapeDtypeStruct((bsz, t, ML_W), F32)] * 2
                  + [jax.ShapeDtypeStruct(a.shape, F32) for a in state],
        scratch_shapes=[pltpu.VMEM((nb,) + a.shape[1:], F32) for a in state],
        compiler_params=_params("parallel", "arbitrary"),
        name="mlstm_scan",
    )(*([qk, qk, proj, gates_r] * 2), gbr, tri_b, trit_b, *state)
    return outs[0], outs[1], tuple(outs[2:])


def _ssd_steps(dtc, dtr, bias_c, bias_r, a_c, a_r, tri_b, trit_b):
    dt_c = _softplus(dtc + bias_c)
    dt_r = _softplus(dtr + bias_r)
    hi, lo = _hi_lo(dt_c * (-LOG2_E * jnp.exp(a_c)))
    c2 = jnp.dot(tri_b, jnp.concatenate([hi, lo], axis=1), preferred_element_type=F32)
    cum_c = c2[:, :LANES] + c2[:, LANES:]
    hi, lo = _hi_lo(dt_r * (-LOG2_E * jnp.exp(a_r)))
    c2 = jnp.dot(jnp.concatenate([hi, lo], axis=0), trit_b, preferred_element_type=F32)
    cum_r = c2[:SSD_HEADS * 2] + c2[SSD_HEADS * 2:]
    return dt_r, cum_c, cum_r


def _ssd_group(x_ref, bg, cg, steps, tri, d, g, s_ref, y_ref, first, first_b):
    L = SSD_CHUNK
    dt_r, cum_c, cum_r = steps
    end = 0 if d else L - 1
    bg_t = bg.astype(F32).T
    if y_ref is not None:
        cb = _dot_nt(cg, bg)
        y_inter = _dot(cg, s_ref[...])
        yield
    for j in range(SSD_HPG // 2):
        ha = d * SSD_HEADS + g * SSD_HPG + 2 * j
        tile = slice(j * LANES, (j + 1) * LANES)
        xt = x_ref[:, tile]
        x2 = jnp.concatenate([xt * first_b, xt * (1 - first_b)], axis=0)
        ws, es, lhs_s, decs = [], [], [], []
        for hh in (ha, ha + 1):
            cum_row, dt_row = cum_r[hh:hh + 1, :], dt_r[hh:hh + 1, :]
            c_end = cum_row[:, end:end + 1]
            lhs_s.append(bg_t * (jnp.exp2(c_end - cum_row) * dt_row))
            decs.append(jnp.exp2(c_end))
            if y_ref is not None:
                cum_col = jnp.broadcast_to(cum_c[:, hh:hh + 1], (L, LANES))
                ws.append(jnp.exp2(jnp.where(tri, cum_col - cum_row, NEG_INF)) * cb * dt_row)
                es.append(jnp.exp2(cum_col))
        if y_ref is not None:
            y_ref[:, tile] = (_dot(jnp.concatenate(ws, axis=1), x2)
                              + jnp.where(first, es[0], es[1]) * y_inter[:, tile]).astype(y_ref.dtype)
        dec = jnp.where(first[0:1], decs[0], decs[1])
        s_ref[:, tile] = dec * s_ref[:, tile] + _dot(jnp.concatenate(lhs_s, axis=1), x2)
        yield


def _ssd_kernel(xf, bmf, cmf, dtcf, dtrf, xb, bmb, cmb, dtcb, dtrb, bias_c_ref, bias_r_ref,
                a_c_ref, a_r_ref, tri_ref, trit_ref, s0_ref, *rest, n_chunks, nb, with_out):
    yf_ref, yb_ref = (rest[0], rest[1]) if with_out else (None, None)
    s_out_ref, s_scr = rest[-2:]
    c = pl.program_id(1)

    @pl.when(c == 0)
    def _():
        s_scr[...] = s0_ref[...]

    first = lax.broadcasted_iota(jnp.int32, (SSD_CHUNK, LANES), 1) < SSD_P
    first_b = jnp.where(first, 1.0, 0.0).astype(BF16)
    chains = []
    for d, (x_ref, bm_ref, cm_ref, dtc_ref, dtr_ref, y_ref) in enumerate(
            ((xf, bmf, cmf, dtcf, dtrf, yf_ref), (xb, bmb, cmb, dtcb, dtrb, yb_ref))):
        tri = tri_ref[d] > 0.5
        for bi in range(nb):
            steps = _ssd_steps(dtc_ref[bi], dtr_ref[bi, 0], bias_c_ref[...], bias_r_ref[...], a_c_ref[...],
                               a_r_ref[...], tri_ref[d], trit_ref[d])
            for g in range(SSD_GROUPS):
                nsl = slice(g * SSD_STATE, (g + 1) * SSD_STATE)
                gsl = slice(g * SSD_HPG * SSD_P, (g + 1) * SSD_HPG * SSD_P)
                chains.append(_ssd_group(x_ref.at[bi, :, gsl], bm_ref[bi, :, nsl], cm_ref[bi, :, nsl], steps,
                                         tri, d, g, s_scr.at[bi, d, g],
                                         None if y_ref is None else y_ref.at[bi, :, gsl], first, first_b))
    _round_robin(chains)

    @pl.when(c == n_chunks - 1)
    def _():
        s_out_ref[...] = s_scr[...]


def _ssd_scan(xbc, proj, dt_r, dt_bias, a_log, s0, with_out):
    bsz, t, _ = xbc.shape
    L, nb = SSD_CHUNK, SSD_NB
    nc = t // L
    tri_b, trit_b = _tri_pair(L)
    nh = 2 * SSD_HEADS
    bias_c = jnp.zeros((1, LANES), F32).at[0, :nh].set(dt_bias.reshape(nh))
    a_c = jnp.zeros((1, LANES), F32).at[0, :nh].set(a_log.reshape(nh))
    bias_r, a_r = dt_bias.reshape(nh, 1), a_log.reshape(nh, 1)
    fwd, bwd = _seq_specs(nb, L, nc)
    gfwd, gbwd = _chunk_specs(nb, nh, L, nc)
    nbc = D_INNER // SSD_BC
    seqs = lambda s, g: [s(D_INNER, 0), s(SSD_BC, nbc), s(SSD_BC, nbc + 1), s(LANES, ODD_IN // LANES), g]
    consts = [bias_c, bias_r, a_c, a_r, tri_b, trit_b]
    out_specs, out_shape = [_state_spec(s0, nb)], [jax.ShapeDtypeStruct(s0.shape, F32)]
    if with_out:
        out_specs = [fwd(D_INNER, 0), bwd(D_INNER, 0)] + out_specs
        out_shape = [jax.ShapeDtypeStruct((bsz, t, D_INNER), BF16)] * 2 + out_shape
    outs = pl.pallas_call(
        functools.partial(_ssd_kernel, n_chunks=nc, nb=nb, with_out=with_out),
        grid=(bsz // nb, nc),
        in_specs=seqs(fwd, gfwd) + seqs(bwd, gbwd) + [_const_spec(a) for a in consts] + [_state_spec(s0, nb)],
        out_specs=out_specs, out_shape=out_shape,
        scratch_shapes=[pltpu.VMEM((nb,) + s0.shape[1:], F32)],
        compiler_params=_params("parallel", "arbitrary"),
        name="ssd_scan",
    )(*([xbc, xbc, xbc, proj, dt_r] * 2), *consts, s0)
    return (outs[0], outs[1], outs[2]) if with_out else (None, None, outs[0])


def _head_rms(y, g, width):
    parts = []
    for s in range(0, y.shape[1], width):
        p = y[:, s:s + width]
        parts.append(p * lax.rsqrt(jnp.mean(p * p, axis=-1, keepdims=True) + EPS))
    return jnp.concatenate(parts, axis=1) * g


def _even_out_kernel(ogf_ref, ogb_ref, hmf_ref, hmb_ref, gg_ref, mo_ref, gn_ref, mn_ref, w_ref,
                     x_ref, gate_ref, o_ref):
    gla = _head_rms(ogf_ref[0] + ogb_ref[0], gn_ref[...], GLA_DV) * _silu(gg_ref[0])
    ml = _head_rms(_sigmoid(mo_ref[0]) * (hmf_ref[0] + hmb_ref[0]), mn_ref[...], ML_D)
    y = _dot(gla, w_ref[:GLA_V]) + _dot(ml, w_ref[GLA_V:])
    o_ref[0] = x_ref[0] + gate_ref[0] * y


def _even_out(o_gla, h_ml, proj, gla_norm_g, ml_norm_g, w_out, x, gate, tm):
    bsz, t, d = x.shape
    tok = lambda w_, i: pl.BlockSpec((1, tm, w_), lambda b, s: (b, s, i))
    const = lambda a: pl.BlockSpec(a.shape, lambda b, s: (0,) * a.ndim)
    gn, mn = gla_norm_g.reshape(1, GLA_V), ml_norm_g.reshape(1, ML_W)
    return pl.pallas_call(
        _even_out_kernel,
        grid=(bsz, t // tm),
        in_specs=[tok(GLA_V, 0)] * 2 + [tok(ML_W, 0)] * 2 + [tok(GLA_V, 2), tok(ML_W, 6), const(gn),
                                                             const(mn), const(w_out), tok(d, 0),
                                                             _bvec_spec(gate)],
        out_specs=tok(d, 0),
        out_shape=jax.ShapeDtypeStruct(x.shape, F32),
        compiler_params=_params("parallel", "parallel"),
        name="even_out",
    )(*o_gla, *h_ml, proj, proj, gn, mn, w_out, x, gate)


def _odd_out_kernel(yf_ref, yb_ref, xs_ref, z_ref, dsk_ref, ng_ref, w_ref, x_ref, gate_ref, o_ref):
    y = yf_ref[0].astype(F32) + yb_ref[0].astype(F32) + dsk_ref[...] * xs_ref[0].astype(F32)
    y = y * _silu(z_ref[0])
    y = _head_rms(y, ng_ref[...], D_INNER // SSD_GROUPS)
    o_ref[0] = x_ref[0] + gate_ref[0] * _dot(y, w_ref[...])


def _odd_out(y, xbc, proj, d_skip, norm_g, w_out, x, gate, tm):
    bsz, t, d = x.shape
    tok = lambda w_, i: pl.BlockSpec((1, tm, w_), lambda b, s: (b, s, i))
    const = lambda a: pl.BlockSpec(a.shape, lambda b, s: (0,) * a.ndim)
    dsk = jnp.repeat(d_skip, SSD_P).reshape(1, D_INNER)
    ng = norm_g.reshape(1, D_INNER)
    return pl.pallas_call(
        _odd_out_kernel,
        grid=(bsz, t // tm),
        in_specs=[tok(D_INNER, 0)] * 4 + [const(dsk), const(ng), const(w_out), tok(d, 0), _bvec_spec(gate)],
        out_specs=tok(d, 0),
        out_shape=jax.ShapeDtypeStruct(x.shape, F32),
        compiler_params=_params("parallel", "parallel"),
        name="odd_out",
    )(*y, xbc, proj, dsk, ng, w_out, x, gate)


FFN_TB = 256
FFN_PAD = GRID_W
FFN_CT = 256


def _ffn_shift_matrix(grid_conv):
    i = np.arange(FFN_TB)[:, None]
    j = np.arange(FFN_TB)[None, :]
    prev, nxt = (j == i - 1), (j == i + 1)
    if grid_conv:
        prev &= (i % GRID_W != 0)
        nxt &= (i % GRID_W != GRID_W - 1)
    return jnp.asarray(np.concatenate([prev, nxt], axis=0), BF16)


def _ffn_conv_block(uc_scr, ul_scr, ur_scr, r0, w9, b, grid_conv):
    acc = b.astype(BF16)
    for dr in ((-1, 0, 1) if grid_conv else (0,)):
        rows = pl.ds(FFN_PAD + r0 + GRID_W * dr, FFN_TB)
        i = 3 * (dr + 1)
        acc = acc + (ul_scr[rows, :] * w9[i:i + 1].astype(BF16) + uc_scr[rows, :] * w9[i + 1:i + 2].astype(BF16)
                     + ur_scr[rows, :] * w9[i + 2:i + 3].astype(BF16))
    return acc


def _ffn_up_kernel(x_ref, g_ref, sc_ref, sh_ref, wa_ref, wg_ref, cwa_ref, cwg_ref, cba_ref, cbg_ref, shift_ref,
                   o_ref, h_scr, *u_scrs, grid_conv):
    t = x_ref.shape[1]
    ct = o_ref.shape[2]

    @pl.when(pl.program_id(1) == 0)
    def _():
        h_scr[...] = _norm_mod(x_ref[0], g_ref[...], sc_ref[0], sh_ref[0]).astype(BF16)
        zeros = jnp.zeros((FFN_PAD, ct), BF16)
        for scr in u_scrs:
            scr[pl.ds(0, FFN_PAD), :] = zeros
            scr[pl.ds(FFN_PAD + t, FFN_PAD), :] = zeros

    halves = ((wa_ref, cwa_ref, cba_ref, u_scrs[0:3]), (wg_ref, cwg_ref, cbg_ref, u_scrs[3:6]))

    def up_block(blk):
        hb = h_scr[pl.ds(blk * FFN_TB, FFN_TB), :]
        for w_ref, _, _, (uc_scr, _, _) in halves:
            uc_scr[pl.ds(FFN_PAD + blk * FFN_TB, FFN_TB), :] = jnp.dot(
                hb, w_ref[...], preferred_element_type=F32).astype(BF16)

    def shift_block(blk):
        rows = pl.ds(FFN_PAD + blk * FFN_TB, FFN_TB)
        for _, _, _, (uc_scr, ul_scr, ur_scr) in halves:
            lr = jnp.dot(shift_ref[...], uc_scr[rows, :], preferred_element_type=F32).astype(BF16)
            ul_scr[rows, :] = lr[:FFN_TB]
            ur_scr[rows, :] = lr[FFN_TB:]

    def conv_block(blk):
        r0 = blk * FFN_TB
        a, g = [_ffn_conv_block(*scrs, r0, cw_ref, cb_ref[...], grid_conv).astype(F32)
                for _, cw_ref, cb_ref, scrs in halves]
        o_ref[0, pl.ds(r0, FFN_TB), :] = (_silu(g) * a).astype(BF16)

    nblk = t // FFN_TB
    for step in range(nblk + 3):
        if step < nblk:
            up_block(step)
        if 0 <= step - 1 < nblk:
            shift_block(step - 1)
        if 0 <= step - 3 < nblk:
            conv_block(step - 3)


def _ffn_up(x, g, sc, sh, w_up, conv_w9, conv_b, grid_conv):
    bsz, t, d = x.shape
    ct = FFN_CT
    nct = D_FF // ct
    cb = conv_b.reshape(1, 2 * D_FF)
    assert t % FFN_TB == 0 and (grid_conv or t == FFN_TB)
    shift = _ffn_shift_matrix(grid_conv)
    return pl.pallas_call(
        functools.partial(_ffn_up_kernel, grid_conv=grid_conv),
        grid=(bsz, nct),
        in_specs=[pl.BlockSpec((1, t, d), lambda b, j: (b, 0, 0)),
                  pl.BlockSpec((1, d), lambda b, j: (0, 0)),
                  _bvec_spec(sc), _bvec_spec(sh),
                  pl.BlockSpec((d, ct), lambda b, j: (0, j)),
                  pl.BlockSpec((d, ct), lambda b, j: (0, nct + j)),
                  pl.BlockSpec((9, ct), lambda b, j: (0, j)),
                  pl.BlockSpec((9, ct), lambda b, j: (0, nct + j)),
                  pl.BlockSpec((1, ct), lambda b, j: (0, j)),
                  pl.BlockSpec((1, ct), lambda b, j: (0, nct + j)),
                  pl.BlockSpec(shift.shape, lambda b, j: (0, 0))],
        out_specs=pl.BlockSpec((1, t, ct), lambda b, j: (b, 0, j)),
        out_shape=jax.ShapeDtypeStruct((bsz, t, D_FF), BF16),
        scratch_shapes=[pltpu.VMEM((t, d), BF16)] + [pltpu.VMEM((t + 2 * FFN_PAD, ct), BF16)] * 6,
        compiler_params=_params("parallel", "arbitrary"),
        name="ffn_up",
    )(x, g.reshape(1, d), sc, sh, w_up, w_up, conv_w9, conv_w9, cb, cb, shift)


def _ffn_down_kernel(a_ref, w_ref, x_ref, gate_ref, fg_ref, o_ref, *, final_norm):
    y = x_ref[0] + gate_ref[0] * jnp.dot(a_ref[0], w_ref[...], preferred_element_type=F32)
    if final_norm:
        y = y * lax.rsqrt(jnp.mean(y * y, axis=-1, keepdims=True) + EPS) * fg_ref[...]
    o_ref[0] = y


def _ffn_down(act, w_down, x, gate, final_g, final_norm, tm):
    bsz, t, d = x.shape
    tok = lambda w_: pl.BlockSpec((1, tm, w_), lambda b, s: (b, s, 0))
    const = lambda a: pl.BlockSpec(a.shape, lambda b, s: (0,) * a.ndim)
    fg = final_g.reshape(1, d)
    return pl.pallas_call(
        functools.partial(_ffn_down_kernel, final_norm=final_norm),
        grid=(bsz, t // tm),
        in_specs=[tok(D_FF), const(w_down), tok(d), _bvec_spec(gate), const(fg)],
        out_specs=tok(d),
        out_shape=jax.ShapeDtypeStruct(x.shape, F32),
        compiler_params=_params("parallel", "parallel"),
        name="ffn_down",
    )(act, w_down, x, gate, fg)


def _rows_to_chunks(a, L):
    bsz, t, ch = a.shape
    return a.reshape(bsz, t // L, L, ch).transpose(0, 1, 3, 2)


def _even_mixer(xc, xl, mods_c, mods_l, norm_g, p):
    (sh1c, sc1c, g1c), (sh1l, sc1l, g1l) = mods_c, mods_l
    w_in = jnp.concatenate([p["w_in"], p["a1"][0], p["a1"][1],
                            jnp.zeros((D_MODEL, EVEN_N - EVEN_IN - 2 * GLA_RANK), F32)], axis=1).astype(BF16)
    mq_off = GLA_QK * 2 + GLA_V * 2
    projs, qks = {}, {}
    for s, (xs, sc1, sh1, tm) in (("c", (xc, sc1c, sh1c, 256)), ("l", (xl, sc1l, sh1l, 512))):
        projs[s], qks[s] = _norm_proj(xs, norm_g, sc1, sh1, w_in, tm, mq_off, p["conv_w"], p["conv_b"])
    gates_r = {s: _rows_to_chunks(projs[s][:, :, EVEN_IN - 16:EVEN_IN], ML_CHUNK) for s in projs}
    bsz = xl.shape[0]

    a2p = jnp.stack([jnp.zeros((LANES, GLA_QK), F32).at[16 + 16 * d:32 + 16 * d].set(p["a2"][d])
                     for d in range(2)]).astype(BF16)
    ab = p["ab"].reshape(2, 1, GLA_QK)
    gla_state = jnp.zeros((bsz, 2, 2, 2 * GLA_DV, LANES), F32)
    ml_state = (jnp.zeros((bsz, 2, ML_HEADS, ML_D, 2 * ML_D), F32), jnp.zeros((bsz, 2, ML_HEADS, 1, LANES), F32))
    o_gla, h_ml = {}, {}
    for s in ("c", "l"):
        of, ob, gla_state = _gla_scan(projs[s], a2p, ab, gla_state)
        o_gla[s] = (of, ob)
        hf, hb, ml_state = _mlstm_scan(qks[s], projs[s], gates_r[s], p["gate_b"], ml_state)
        h_ml[s] = (hf, hb)

    w_out = p["w_out"].astype(BF16)
    xc = _even_out(o_gla["c"], h_ml["c"], projs["c"], p["gla_norm_g"], p["ml_norm_g"], w_out, xc, g1c, 256)
    xl = _even_out(o_gla["l"], h_ml["l"], projs["l"], p["gla_norm_g"], p["ml_norm_g"], w_out, xl, g1l, 512)
    return xc, xl


def _odd_mixer_last(xc, xl, mods_c, mods_l, norm_g, p):
    (sh1c, sc1c, _), (sh1l, sc1l, g1l) = mods_c, mods_l
    w_in = jnp.concatenate([p["w_in"], jnp.zeros((D_MODEL, ODD_N - ODD_IN), F32)], axis=1).astype(BF16)
    projs, xbcs = {}, {}
    for s, (xs, sc1, sh1) in (("c", (xc, sc1c, sh1c)), ("l", (xl, sc1l, sh1l))):
        projs[s], xbcs[s] = _norm_proj(xs, norm_g, sc1, sh1, w_in, 256, D_INNER, p["conv_w"], p["conv_b"])
    dt_rs = {s: _rows_to_chunks(projs[s][:, :, ODD_IN - 2 * SSD_HEADS:ODD_IN], SSD_CHUNK) for s in projs}
    bsz = xl.shape[0]
    state = jnp.zeros((bsz, 2, SSD_GROUPS, SSD_STATE, SSD_HPG * SSD_P), F32)
    _, _, state = _ssd_scan(xbcs["c"], projs["c"], dt_rs["c"], p["dt_bias"], p["a_log"], state, False)
    yf, yb, _ = _ssd_scan(xbcs["l"], projs["l"], dt_rs["l"], p["dt_bias"], p["a_log"], state, True)
    return _odd_out((yf, yb), xbcs["l"], projs["l"], p["d_skip"], p["norm_g"], p["w_out"].astype(BF16),
                    xl, g1l, 512)


def kernel(x, c, ctx, c_ctx, mod_w, mod_b, norm_mix_g, norm_ffn_g, final_norm_g, ffn_w_up, ffn_conv_w, ffn_conv_b, ffn_w_down, even_w_in, gla_a1, gla_a2, gla_ab, ml_conv_w, ml_conv_b, ml_gate_b, gla_norm_g, ml_norm_g, even_w_out, ssd_w_in, ssd_conv_w, ssd_conv_b, ssd_dt_bias, ssd_a_log, ssd_d, ssd_norm_g, ssd_w_out):
    depth = mod_w.shape[0]
    bsz = x.shape[0]
    assert depth == 2 and x.shape[1] % (GRID_W * SUBLANES) == 0
    assert bsz % GLA_NB == 0 and bsz % ML_NB == 0 and bsz % SSD_NB == 0
    rows = -(-(bsz + 1) // SUBLANES) * SUBLANES
    cc = jnp.zeros((rows, D_MODEL), F32).at[:bsz].set(c).at[bsz].set(c_ctx)
    mod = _modulation(cc, mod_w, mod_b)

    def mods(layer, lo, hi):
        return [mod[layer, lo:hi, i * D_MODEL:(i + 1) * D_MODEL][:, None, :] for i in range(6)]

    xl, xc = x, ctx
    for layer in range(depth):
        last = layer == depth - 1
        sh1l, sc1l, g1l, sh2l, sc2l, g2l = mods(layer, 0, bsz)
        sh1c, sc1c, g1c, sh2c, sc2c, g2c = mods(layer, bsz, bsz + 1)
        if layer == 0:
            p = dict(w_in=even_w_in[0], a1=gla_a1[0], a2=gla_a2[0], ab=gla_ab[0], conv_w=ml_conv_w[0],
                     conv_b=ml_conv_b[0], gate_b=ml_gate_b[0], gla_norm_g=gla_norm_g[0],
                     ml_norm_g=ml_norm_g[0], w_out=even_w_out[0])
            xc, xl = _even_mixer(xc, xl, (sh1c, sc1c, g1c), (sh1l, sc1l, g1l), norm_mix_g[layer], p)
        else:
            p = dict(w_in=ssd_w_in[0], conv_w=ssd_conv_w[0], conv_b=ssd_conv_b[0], dt_bias=ssd_dt_bias[0],
                     a_log=ssd_a_log[0], d_skip=ssd_d[0], norm_g=ssd_norm_g[0], w_out=ssd_w_out[0])
            xl = _odd_mixer_last(xc, xl, (sh1c, sc1c, g1c), (sh1l, sc1l, g1l), norm_mix_g[layer], p)
        w_up = ffn_w_up[layer].astype(BF16)
        w_down = ffn_w_down[layer].astype(BF16)
        cw9 = ffn_conv_w[layer].reshape(9, 2 * D_FF)
        act = _ffn_up(xl, norm_ffn_g[layer], sc2l, sh2l, w_up, cw9, ffn_conv_b[layer], True)
        xl = _ffn_down(act, w_down, xl, g2l, final_norm_g, last, 1024)
        if not last:
            act = _ffn_up(xc, norm_ffn_g[layer], sc2c, sh2c, w_up, cw9, ffn_conv_b[layer], False)
            xc = _ffn_down(act, w_down, xc, g2c, final_norm_g, False, 256)
    return xl
```

```python
import functools

import numpy as np
import jax
import jax.numpy as jnp
from jax import lax
from jax.experimental import pallas as pl
from jax.experimental.pallas import tpu as pltpu

F32 = jnp.float32
BF16 = jnp.bfloat16

D_MODEL = 1024
GRID_W = 64
EPS = 1e-6

GLA_HEADS, GLA_DK, GLA_DV, GLA_RANK, GLA_TAU, GLA_CHUNK = 4, 64, 128, 16, 16.0, 64
ML_HEADS, ML_D, ML_CHUNK = 4, 128, 64
D_INNER, SSD_HEADS, SSD_GROUPS, SSD_HPG, SSD_P, SSD_STATE, SSD_CHUNK = 2048, 32, 4, 8, 64, 128, 128
D_FF = 2816
GLA_QK, GLA_V, ML_W = GLA_HEADS * GLA_DK, GLA_HEADS * GLA_DV, ML_HEADS * ML_D
EVEN_IN = 3600
ODD_IN = 5184
SSD_BC = SSD_GROUPS * SSD_STATE
SSD_CONV_CH = D_INNER + 2 * SSD_BC

LANES = 128
SUBLANES = 8
EVEN_N = 3712
ODD_N = 5248
V7X_VMEM_BYTES = 64 * 1024 * 1024
VMEM_LIMIT = V7X_VMEM_BYTES - 8 * 1024 * 1024

GLA_NB = 4
ML_NB = 4
SSD_NB = 2

NEG_INF = float("-inf")
LOG2_E = 1.4426950408889634


def _params(*sem):
    return pltpu.CompilerParams(dimension_semantics=sem, vmem_limit_bytes=VMEM_LIMIT)


def _sigmoid(x):
    return 1.0 / (1.0 + jnp.exp(-x))


def _silu(x):
    return x * _sigmoid(x)


def _softplus(x):
    return jnp.maximum(x, 0.0) + jnp.log(1.0 + jnp.exp(-jnp.abs(x)))


def _log_sigmoid(x):
    return -_softplus(-x)


def _hi_lo(x):
    hi = x.astype(BF16)
    lo = (x - hi.astype(F32)).astype(BF16)
    return hi, lo


def _dot(a, b):
    return jnp.dot(a.astype(BF16), b.astype(BF16), preferred_element_type=F32)


def _dot_nt(a, b):
    return lax.dot_general(a.astype(BF16), b.astype(BF16), (((1,), (1,)), ((), ())),
                           preferred_element_type=F32)


def _norm_mod(x, g, sc, sh):
    y = x * lax.rsqrt(jnp.mean(x * x, axis=-1, keepdims=True) + EPS)
    return (y * g) * (1.0 + sc) + sh


def _mod_kernel(c_ref, w_ref, b_ref, o_ref):
    s = _silu(c_ref[...])
    hi, lo = _hi_lo(s)
    w = w_ref[0]
    whi, wlo = _hi_lo(w)
    acc = jnp.dot(hi, whi, preferred_element_type=F32)
    acc += jnp.dot(lo, whi, preferred_element_type=F32)
    acc += jnp.dot(hi, wlo, preferred_element_type=F32)
    o_ref[0] = acc + b_ref[0]


def _modulation(cc, mod_w, mod_b):
    depth, d, n = mod_w.shape
    rows = cc.shape[0]
    tn = 1536
    return pl.pallas_call(
        _mod_kernel,
        grid=(depth, n // tn),
        in_specs=[pl.BlockSpec((rows, d), lambda l, j: (0, 0)),
                  pl.BlockSpec((1, d, tn), lambda l, j: (l, 0, j)),
                  pl.BlockSpec((1, 1, tn), lambda l, j: (l, 0, j))],
        out_specs=pl.BlockSpec((1, rows, tn), lambda l, j: (l, 0, j)),
        out_shape=jax.ShapeDtypeStruct((depth, rows, n), F32),
        compiler_params=_params("parallel", "parallel"),
        name="modulation",
    )(cc, mod_w, mod_b.reshape(depth, 1, n))


def _norm_proj_kernel(x_ref, xp_ref, xn_ref, g_ref, sc_ref, sh_ref, w_ref, cw_ref, cb_ref, o_ref, oc_ref, *,
                      conv_col0):
    i = pl.program_id(1)
    tm = x_ref.shape[1]
    norm = lambda x: _norm_mod(x, g_ref[...], sc_ref[0], sh_ref[0]).astype(BF16)
    hb = norm(x_ref[0])
    hb_halo = jnp.concatenate([hb, norm(xp_ref[0]), norm(xn_ref[0])], axis=0)
    row = lax.broadcasted_iota(jnp.int32, (tm, LANES), 0)
    first_tile, last_tile = i == 0, i == pl.num_programs(1) - 1
    n = w_ref.shape[1]
    ch = oc_ref.shape[2]
    chunk = 1024
    halos = {}

    def conv_slice(s):
        u = o_ref[0, :, conv_col0 + s:conv_col0 + s + LANES]
        uh = halos.pop(s)
        before = jnp.where(first_tile, 0.0, uh[SUBLANES - 1:SUBLANES])
        after = jnp.where(last_tile, 0.0, uh[SUBLANES:SUBLANES + 1])
        prev = jnp.where(row == 0, before, pltpu.roll(u, 1, 0))
        nxt = jnp.where(row == tm - 1, after, pltpu.roll(u, tm - 1, 0))
        cw = cw_ref[:, s:s + LANES]
        y = prev * cw[0:1] + u * cw[1:2] + nxt * cw[2:3] + cb_ref[:, s:s + LANES]
        oc_ref[0, :, s:s + LANES] = _silu(y).astype(oc_ref.dtype)

    pending = list(range(0, ch, LANES))
    for s in range(0, n, chunk):
        e = min(s + chunk, n)
        with_halo = s < conv_col0 + ch and e > conv_col0
        r = jnp.dot(hb_halo if with_halo else hb, w_ref[:, s:e], preferred_element_type=F32)
        o_ref[0, :, s:e] = r[:tm]
        for c in range(max(s, conv_col0), min(e, conv_col0 + ch), LANES):
            halos[c - conv_col0] = r[tm:, c - s:c - s + LANES]
        while pending and conv_col0 + pending[0] + LANES <= s:
            conv_slice(pending.pop(0))
    for s in pending:
        conv_slice(s)


def _bvec_spec(v):
    d = v.shape[-1]
    if v.shape[0] == 1:
        return pl.BlockSpec((1, 1, d), lambda b, t: (0, 0, 0))
    return pl.BlockSpec((1, 1, d), lambda b, t: (b, 0, 0))


def _norm_proj(x, g, sc, sh, w, tm, conv_col0, conv_w, conv_b):
    bsz, t, d = x.shape
    n = w.shape[1]
    ch = conv_w.shape[1]
    hb = tm // SUBLANES
    last = t // SUBLANES - 1
    return pl.pallas_call(
        functools.partial(_norm_proj_kernel, conv_col0=conv_col0),
        grid=(bsz, t // tm),
        in_specs=[pl.BlockSpec((1, tm, d), lambda b, i: (b, i, 0)),
                  pl.BlockSpec((1, SUBLANES, d), lambda b, i: (b, jnp.maximum(i * hb - 1, 0), 0)),
                  pl.BlockSpec((1, SUBLANES, d), lambda b, i: (b, jnp.minimum((i + 1) * hb, last), 0)),
                  pl.BlockSpec((1, d), lambda b, i: (0, 0)),
                  _bvec_spec(sc), _bvec_spec(sh),
                  pl.BlockSpec((d, n), lambda b, i: (0, 0), pipeline_mode=pl.Buffered(1)),
                  pl.BlockSpec((3, ch), lambda b, i: (0, 0)),
                  pl.BlockSpec((1, ch), lambda b, i: (0, 0))],
        out_specs=[pl.BlockSpec((1, tm, n), lambda b, i: (b, i, 0)),
                   pl.BlockSpec((1, tm, ch), lambda b, i: (b, i, 0))],
        out_shape=[jax.ShapeDtypeStruct((bsz, t, n), F32), jax.ShapeDtypeStruct((bsz, t, ch), BF16)],
        compiler_params=_params("parallel", "parallel"),
        name="norm_proj",
    )(x, x, x, g.reshape(1, d), sc, sh, w, conv_w, conv_b.reshape(1, ch))


def _flip2(a):
    return a[..., ::-1, ::-1].copy()


GLA_LEVELS = (32, 16, 8, 4, 2, 1)
GLA_U_ROWS = (2 + len(GLA_LEVELS)) * GLA_CHUNK + SUBLANES


def _gla_constants():
    L = GLA_CHUNK
    t = np.arange(L)[:, None]
    j = np.arange(L)[None, :]
    blocks = [(j <= t), (j > t)]
    masks = [np.eye(L)]
    for m in GLA_LEVELS:
        pos, blk = t % (2 * m), t // (2 * m)
        ref = blk * 2 * m + m - 1
        blocks.append(((pos >= m) & (j > ref) & (j <= t)) | ((pos < m) & (j > t) & (j <= ref)))
        masks.append((blk == blk.T) & (pos >= m) & (pos.T < m))
    blocks = [np.asarray(b_, np.float32) for b_ in blocks]
    masks = [np.asarray(m_, np.float32) for m_ in masks]
    us, mask2s = [], []
    for reverse in (False, True):
        bl = [_flip2(b_) for b_ in blocks] if reverse else blocks
        ms = [_flip2(m_) for m_ in masks] if reverse else masks
        us.append(np.concatenate(bl + [np.ones((SUBLANES, L), np.float32)], axis=0))
        mask2s.append(np.stack([np.concatenate([m_, m_], axis=0) for m_ in ms]))
    return jnp.asarray(np.stack(us), BF16), jnp.asarray(np.stack(mask2s), F32)


def _tri_pair(L):
    t = np.arange(L)[:, None]
    j = np.arange(L)[None, :]
    tri = np.stack([np.asarray(j <= t, np.float32), np.asarray(j >= t, np.float32)])
    return jnp.asarray(tri, BF16), jnp.asarray(tri.transpose(0, 2, 1), BF16)


def _const_spec(a):
    return pl.BlockSpec(a.shape, lambda b, c: (0,) * a.ndim)


def _state_spec(a, nb):
    return pl.BlockSpec((nb,) + a.shape[1:], lambda b, c: (b,) + (0,) * (a.ndim - 1))


def _seq_specs(nb, L, nc):
    fwd = lambda w_, i: pl.BlockSpec((nb, L, w_), lambda b, c: (b, c, i))
    bwd = lambda w_, i: pl.BlockSpec((nb, L, w_), lambda b, c: (b, nc - 1 - c, i))
    return fwd, bwd


def _chunk_specs(nb, rows, L, nc):
    fwd = pl.BlockSpec((nb, 1, rows, L), lambda b, c: (b, c, 0, 0))
    bwd = pl.BlockSpec((nb, 1, rows, L), lambda b, c: (b, nc - 1 - c, 0, 0))
    return fwd, bwd


def _round_robin(chains):
    chains = list(chains)
    while chains:
        alive = []
        for ch in chains:
            try:
                next(ch)
                alive.append(ch)
            except StopIteration:
                pass
        chains = alive


def _gla_tile(q, k, vj, e, mask_ref, d, st_ref, o_ref, first, same_head):
    L = GLA_CHUNK
    cum, e_end, c_end = e[0:L], e[L:2 * L], e[GLA_U_ROWS - SUBLANES:GLA_U_ROWS - SUBLANES + 1]
    q = q * (GLA_DK ** -0.5)
    vj_t = vj.T
    q_in = q * jnp.exp(cum)
    k_end = k * jnp.exp(e_end)
    st = st_ref[...]
    o_inter = _dot_nt(q_in, st)
    att = jnp.zeros((2 * L, L), F32)
    for lvl in range(7):
        x = None if lvl == 0 else jnp.exp(e[(1 + lvl) * L:(2 + lvl) * L])
        ql = q if lvl == 0 else q * x
        kl = k if lvl == 0 else k * x
        lhs = jnp.concatenate([jnp.where(first, ql, 0.0), jnp.where(first, 0.0, ql)], axis=0)
        att = att + _dot_nt(lhs, kl) * mask_ref[d, lvl]
        yield
    o_a = _dot(att[:L], vj[:, :GLA_DV])
    o_b = _dot(att[L:], vj[:, GLA_DV:])
    upd = _dot(vj_t, k_end)
    yield
    o_ref[...] = jnp.concatenate([o_a, o_b], axis=1) + o_inter
    st_ref[...] = st * jnp.exp(c_end) + jnp.where(same_head, upd, 0.0)
    yield


def _gla_kernel(qf, kf, vf, lrf, qb, kb, vb, lrb, a2_ref, ab_ref, u_ref, mask_ref, s0_ref,
                of_ref, ob_ref, s_out_ref, s_scr, *, n_chunks, nb):
    c = pl.program_id(1)

    @pl.when(c == 0)
    def _():
        s_scr[...] = s0_ref[...]

    first = lax.broadcasted_iota(jnp.int32, (GLA_CHUNK, LANES), 1) < GLA_DK
    row = lax.broadcasted_iota(jnp.int32, (2 * GLA_DV, LANES), 0)
    lane = lax.broadcasted_iota(jnp.int32, (2 * GLA_DV, LANES), 1)
    same_head = (row < GLA_DV) == (lane < GLA_DK)
    sides = ((qf, kf, vf, lrf, of_ref), (qb, kb, vb, lrb, ob_ref))
    samples = [(d, bi) for d in range(2) for bi in range(nb)]
    las = [_log_sigmoid(_dot(sides[d][3][bi], a2_ref[d]) + ab_ref[d]) * (1.0 / GLA_TAU)
           for d, bi in samples]
    es = []
    for (d, bi), la in zip(samples, las):
        hi, lo = _hi_lo(la)
        e2 = jnp.dot(u_ref[d], jnp.concatenate([hi, lo], axis=1), preferred_element_type=F32)
        es.append(e2[:, :GLA_QK] + e2[:, GLA_QK:])
    chains = []
    for (d, bi), e in zip(samples, es):
        q_ref, k_ref, v_ref, _, o_ref = sides[d]
        for j in range(2):
            sl = slice(j * LANES, (j + 1) * LANES)
            vsl = slice(2 * j * GLA_DV, (2 * j + 2) * GLA_DV)
            chains.append(_gla_tile(q_ref[bi, :, sl], k_ref[bi, :, sl], v_ref[bi, :, vsl], e[:, sl], mask_ref,
                                    d, s_scr.at[bi, d, j], o_ref.at[bi, :, vsl], first, same_head))
    _round_robin(chains)

    @pl.when(c == n_chunks - 1)
    def _():
        s_out_ref[...] = s_scr[...]


def _gla_scan(proj, a2p, ab, s0):
    bsz, t, _ = proj.shape
    L, nb = GLA_CHUNK, GLA_NB
    nc = t // L
    u, mask2 = _gla_constants()
    fwd, bwd = _seq_specs(nb, L, nc)
    seqs = lambda s: [s(GLA_QK, 0), s(GLA_QK, 1), s(GLA_V, 1), s(LANES, 28)]
    return pl.pallas_call(
        functools.partial(_gla_kernel, n_chunks=nc, nb=nb),
        grid=(bsz // nb, nc),
        in_specs=seqs(fwd) + seqs(bwd) + [_const_spec(a2p), _const_spec(ab), _const_spec(u),
                                          _const_spec(mask2), _state_spec(s0, nb)],
        out_specs=[fwd(GLA_V, 0), bwd(GLA_V, 0), _state_spec(s0, nb)],
        out_shape=[jax.ShapeDtypeStruct((bsz, t, GLA_V), F32)] * 2 + [jax.ShapeDtypeStruct(s0.shape, F32)],
        scratch_shapes=[pltpu.VMEM((nb,) + s0.shape[1:], F32)],
        compiler_params=_params("parallel", "arbitrary"),
        name="gla_scan",
    )(*([proj] * 8), a2p, ab, u, mask2, s0)


def _mlstm_rows(gr, gbr, trit_b):
    gates_r = gr + gbr
    hi, lo = _hi_lo(_log_sigmoid(gates_r))
    b2 = jnp.dot(jnp.concatenate([hi, lo], axis=0), trit_b, preferred_element_type=F32)
    return gates_r, b2[:16] + b2[16:]


def _running_max(g_r, reverse):
    L = g_r.shape[1]
    x = jnp.concatenate([g_r, jnp.full((1, LANES - L), NEG_INF, F32)], axis=1)
    sh = 1
    while sh < L:
        x = jnp.maximum(x, pltpu.roll(x, LANES - sh if reverse else sh, 1))
        sh *= 2
    return x[:, :L]


def _diag_hi_lo(row, eye2):
    hi, lo = _hi_lo(row)
    return jnp.where(eye2, jnp.concatenate([hi, lo], axis=1).astype(F32), 0.0)


def _mlstm_head(q, k, v, rows, tri, eye2, ones_b, d, h, cn_ref, m_ref, o_ref):
    L = ML_CHUNK
    gates_r, b_rows = rows
    end = 0 if d else L - 1
    ci_, cf_ = 8 * d + h, 8 * d + 4 + h
    b_r, li_r = b_rows[cf_:cf_ + 1, :], gates_r[ci_:ci_ + 1, :]
    b_end = b_r[:, end:end + 1]
    m = m_ref[h][:, 0:1]
    cn = cn_ref[h]
    qk = _dot_nt(q, k) * (ML_D ** -0.5)
    q_cn = _dot(q, cn)
    g_r = li_r - b_r
    mt_r = b_r + jnp.maximum(m, _running_max(g_r, bool(d)))
    d1_r = b_r - mt_r
    lhs = jnp.concatenate([_diag_hi_lo(d1_r, eye2), _diag_hi_lo(mt_r, eye2)], axis=0)
    m_new = jnp.maximum(b_end + m, jnp.max(b_end + g_r, axis=1, keepdims=True))
    decay = jnp.exp(b_end + m - m_new)
    kw_t = k.astype(F32).T * (jnp.exp(b_end + g_r - m_new) * (ML_D ** -0.5))
    v1 = jnp.concatenate([v.astype(BF16), ones_b[:L]], axis=1)
    yield
    cols = jnp.dot(lhs.astype(BF16), ones_b, preferred_element_type=F32)
    yield
    d1_c, mt_c = cols[:L], cols[L:]
    sc = qk * jnp.exp(jnp.where(tri, d1_c[:, :L] + g_r, NEG_INF))
    s_v = jnp.dot(sc.astype(BF16), v1, preferred_element_type=F32)
    yield
    aw = jnp.exp(d1_c + m)
    num = aw * q_cn[:, :ML_D] + s_v[:, :ML_D]
    den = aw * q_cn[:, ML_D:] + s_v[:, ML_D:]
    o_ref[...] = num / jnp.maximum(jnp.abs(den), jnp.exp(-mt_c))
    cn_ref[h] = decay * cn + jnp.dot(kw_t.astype(BF16), v1, preferred_element_type=F32)
    m_ref[h] = jnp.broadcast_to(m_new, (1, LANES))
    yield


def _mlstm_kernel(qf, kf, vf, grf, qb, kb, vb, grb, gbr_ref, tri_ref, trit_ref, cn0_ref, m0_ref,
                  of_ref, ob_ref, cn_out_ref, m_out_ref, cn_scr, m_scr, *, n_chunks, nb):
    c = pl.program_id(1)
    L = ML_CHUNK

    @pl.when(c == 0)
    def _():
        cn_scr[...] = cn0_ref[...]
        m_scr[...] = m0_ref[...]

    ones_b = jnp.ones((2 * L, LANES), BF16)
    eye2 = (lax.broadcasted_iota(jnp.int32, (L, 2 * L), 1) % L) == lax.broadcasted_iota(jnp.int32, (L, 2 * L), 0)
    chains = []
    for d, (q_ref, k_ref, v_ref, gr_ref, o_ref) in enumerate(((qf, kf, vf, grf, of_ref),
                                                              (qb, kb, vb, grb, ob_ref))):
        tri = tri_ref[d] > 0.5
        for bi in range(nb):
            rows = _mlstm_rows(gr_ref[bi, 0], gbr_ref[...], trit_ref[d])
            for h in range(ML_HEADS):
                sl = slice(h * ML_D, (h + 1) * ML_D)
                chains.append(_mlstm_head(q_ref[bi, :, sl], k_ref[bi, :, sl], v_ref[bi, :, sl], rows, tri, eye2,
                                          ones_b, d, h, cn_scr.at[bi, d], m_scr.at[bi, d], o_ref.at[bi, :, sl]))
    _round_robin(chains)

    @pl.when(c == n_chunks - 1)
    def _():
        cn_out_ref[...] = cn_scr[...]
        m_out_ref[...] = m_scr[...]


def _mlstm_scan(qk, proj, gates_r, gate_b, state):
    bsz, t, _ = proj.shape
    L, nb = ML_CHUNK, ML_NB
    nc = t // L
    tri_b, trit_b = _tri_pair(L)
    gbr = gate_b.reshape(16, 1)
    fwd, bwd = _seq_specs(nb, L, nc)
    gfwd, gbwd = _chunk_specs(nb, 16, L, nc)
    seqs = lambda s, g: [s(ML_W, 0), s(ML_W, 1), s(ML_W, 5), g]
    st_specs = [_state_spec(a, nb) for a in state]
    outs = pl.pallas_call(
        functools.partial(_mlstm_kernel, n_chunks=nc, nb=nb),
        grid=(bsz // nb, nc),
        in_specs=seqs(fwd, gfwd) + seqs(bwd, gbwd) + [_const_spec(gbr), _const_spec(tri_b),
                                                      _const_spec(trit_b)] + st_specs,
        out_specs=[fwd(ML_W, 0), bwd(ML_W, 0)] + st_specs,
        out_shape=[jax.ShapeDtypeStruct((bsz, t, ML_W), F32)] * 2
                  + [jax.ShapeDtypeStruct(a.shape, F32) for a in state],
        scratch_shapes=[pltpu.VMEM((nb,) + a.shape[1:], F32) for a in state],
        compiler_params=_params("parallel", "arbitrary"),
        name="mlstm_scan",
    )(*([qk, qk, proj, gates_r] * 2), gbr, tri_b, trit_b, *state)
    return outs[0], outs[1], tuple(outs[2:])


def _ssd_steps(dtc, dtr, bias_c, bias_r, a_c, a_r, tri_b, trit_b):
    dt_c = _softplus(dtc + bias_c)
    dt_r = _softplus(dtr + bias_r)
    hi, lo = _hi_lo(dt_c * (-LOG2_E * jnp.exp(a_c)))
    c2 = jnp.dot(tri_b, jnp.concatenate([hi, lo], axis=1), preferred_element_type=F32)
    cum_c = c2[:, :LANES] + c2[:, LANES:]
    hi, lo = _hi_lo(dt_r * (-LOG2_E * jnp.exp(a_r)))
    c2 = jnp.dot(jnp.concatenate([hi, lo], axis=0), trit_b, preferred_element_type=F32)
    cum_r = c2[:SSD_HEADS * 2] + c2[SSD_HEADS * 2:]
    return dt_r, cum_c, cum_r


def _ssd_group(x_ref, bg, cg, steps, tri, d, g, s_ref, y_ref, first, first_b):
    L = SSD_CHUNK
    dt_r, cum_c, cum_r = steps
    end = 0 if d else L - 1
    bg_t = bg.astype(F32).T
    if y_ref is not None:
        cb = _dot_nt(cg, bg)
        y_inter = _dot(cg, s_ref[...])
        yield
    for j in range(SSD_HPG // 2):
        ha = d * SSD_HEADS + g * SSD_HPG + 2 * j
        tile = slice(j * LANES, (j + 1) * LANES)
        xt = x_ref[:, tile]
        x2 = jnp.concatenate([xt * first_b, xt * (1 - first_b)], axis=0)
        ws, es, lhs_s, decs = [], [], [], []
        for hh in (ha, ha + 1):
            cum_row, dt_row = cum_r[hh:hh + 1, :], dt_r[hh:hh + 1, :]
            c_end = cum_row[:, end:end + 1]
            lhs_s.append(bg_t * (jnp.exp2(c_end - cum_row) * dt_row))
            decs.append(jnp.exp2(c_end))
            if y_ref is not None:
                cum_col = jnp.broadcast_to(cum_c[:, hh:hh + 1], (L, LANES))
                ws.append(jnp.exp2(jnp.where(tri, cum_col - cum_row, NEG_INF)) * cb * dt_row)
                es.append(jnp.exp2(cum_col))
        if y_ref is not None:
            y_ref[:, tile] = (_dot(jnp.concatenate(ws, axis=1), x2)
                              + jnp.where(first, es[0], es[1]) * y_inter[:, tile]).astype(y_ref.dtype)
        dec = jnp.where(first[0:1], decs[0], decs[1])
        s_ref[:, tile] = dec * s_ref[:, tile] + _dot(jnp.concatenate(lhs_s, axis=1), x2)
        yield


def _ssd_kernel(xf, bmf, cmf, dtcf, dtrf, xb, bmb, cmb, dtcb, dtrb, bias_c_ref, bias_r_ref,
                a_c_ref, a_r_ref, tri_ref, trit_ref, s0_ref, *rest, n_chunks, nb, with_out):
    yf_ref, yb_ref = (rest[0], rest[1]) if with_out else (None, None)
    s_out_ref, s_scr = rest[-2:]
    c = pl.program_id(1)

    @pl.when(c == 0)
    def _():
        s_scr[...] = s0_ref[...]

    first = lax.broadcasted_iota(jnp.int32, (SSD_CHUNK, LANES), 1) < SSD_P
    first_b = jnp.where(first, 1.0, 0.0).astype(BF16)
    chains = []
    for d, (x_ref, bm_ref, cm_ref, dtc_ref, dtr_ref, y_ref) in enumerate(
            ((xf, bmf, cmf, dtcf, dtrf, yf_ref), (xb, bmb, cmb, dtcb, dtrb, yb_ref))):
        tri = tri_ref[d] > 0.5
        for bi in range(nb):
            steps = _ssd_steps(dtc_ref[bi], dtr_ref[bi, 0], bias_c_ref[...], bias_r_ref[...], a_c_ref[...],
                               a_r_ref[...], tri_ref[d], trit_ref[d])
            for g in range(SSD_GROUPS):
                nsl = slice(g * SSD_STATE, (g + 1) * SSD_STATE)
                gsl = slice(g * SSD_HPG * SSD_P, (g + 1) * SSD_HPG * SSD_P)
                chains.append(_ssd_group(x_ref.at[bi, :, gsl], bm_ref[bi, :, nsl], cm_ref[bi, :, nsl], steps,
                                         tri, d, g, s_scr.at[bi, d, g],
                                         None if y_ref is None else y_ref.at[bi, :, gsl], first, first_b))
    _round_robin(chains)

    @pl.when(c == n_chunks - 1)
    def _():
        s_out_ref[...] = s_scr[...]


def _ssd_scan(xbc, proj, dt_r, dt_bias, a_log, s0, with_out):
    bsz, t, _ = xbc.shape
    L, nb = SSD_CHUNK, SSD_NB
    nc = t // L
    tri_b, trit_b = _tri_pair(L)
    nh = 2 * SSD_HEADS
    bias_c = jnp.zeros((1, LANES), F32).at[0, :nh].set(dt_bias.reshape(nh))
    a_c = jnp.zeros((1, LANES), F32).at[0, :nh].set(a_log.reshape(nh))
    bias_r, a_r = dt_bias.reshape(nh, 1), a_log.reshape(nh, 1)
    fwd, bwd = _seq_specs(nb, L, nc)
    gfwd, gbwd = _chunk_specs(nb, nh, L, nc)
    nbc = D_INNER // SSD_BC
    seqs = lambda s, g: [s(D_INNER, 0), s(SSD_BC, nbc), s(SSD_BC, nbc + 1), s(LANES, ODD_IN // LANES), g]
    consts = [bias_c, bias_r, a_c, a_r, tri_b, trit_b]
    out_specs, out_shape = [_state_spec(s0, nb)], [jax.ShapeDtypeStruct(s0.shape, F32)]
    if with_out:
        out_specs = [fwd(D_INNER, 0), bwd(D_INNER, 0)] + out_specs
        out_shape = [jax.ShapeDtypeStruct((bsz, t, D_INNER), BF16)] * 2 + out_shape
    outs = pl.pallas_call(
        functools.partial(_ssd_kernel, n_chunks=nc, nb=nb, with_out=with_out),
        grid=(bsz // nb, nc),
        in_specs=seqs(fwd, gfwd) + seqs(bwd, gbwd) + [_const_spec(a) for a in consts] + [_state_spec(s0, nb)],
        out_specs=out_specs, out_shape=out_shape,
        scratch_shapes=[pltpu.VMEM((nb,) + s0.shape[1:], F32)],
        compiler_params=_params("parallel", "arbitrary"),
        name="ssd_scan",
    )(*([xbc, xbc, xbc, proj, dt_r] * 2), *consts, s0)
    return (outs[0], outs[1], outs[2]) if with_out else (None, None, outs[0])


def _head_rms(y, g, width):
    parts = []
    for s in range(0, y.shape[1], width):
        p = y[:, s:s + width]
        parts.append(p * lax.rsqrt(jnp.mean(p * p, axis=-1, keepdims=True) + EPS))
    return jnp.concatenate(parts, axis=1) * g


def _even_out_kernel(ogf_ref, ogb_ref, hmf_ref, hmb_ref, gg_ref, mo_ref, gn_ref, mn_ref, w_ref,
                     x_ref, gate_ref, o_ref):
    gla = _head_rms(ogf_ref[0] + ogb_ref[0], gn_ref[...], GLA_DV) * _silu(gg_ref[0])
    ml = _head_rms(_sigmoid(mo_ref[0]) * (hmf_ref[0] + hmb_ref[0]), mn_ref[...], ML_D)
    y = _dot(gla, w_ref[:GLA_V]) + _dot(ml, w_ref[GLA_V:])
    o_ref[0] = x_ref[0] + gate_ref[0] * y


def _even_out(o_gla, h_ml, proj, gla_norm_g, ml_norm_g, w_out, x, gate, tm):
    bsz, t, d = x.shape
    tok = lambda w_, i: pl.BlockSpec((1, tm, w_), lambda b, s: (b, s, i))
    const = lambda a: pl.BlockSpec(a.shape, lambda b, s: (0,) * a.ndim)
    gn, mn = gla_norm_g.reshape(1, GLA_V), ml_norm_g.reshape(1, ML_W)
    return pl.pallas_call(
        _even_out_kernel,
        grid=(bsz, t // tm),
        in_specs=[tok(GLA_V, 0)] * 2 + [tok(ML_W, 0)] * 2 + [tok(GLA_V, 2), tok(ML_W, 6), const(gn),
                                                             const(mn), const(w_out), tok(d, 0),
                                                             _bvec_spec(gate)],
        out_specs=tok(d, 0),
        out_shape=jax.ShapeDtypeStruct(x.shape, F32),
        compiler_params=_params("parallel", "parallel"),
        name="even_out",
    )(*o_gla, *h_ml, proj, proj, gn, mn, w_out, x, gate)


def _odd_out_kernel(yf_ref, yb_ref, xs_ref, z_ref, dsk_ref, ng_ref, w_ref, x_ref, gate_ref, o_ref):
    y = yf_ref[0].astype(F32) + yb_ref[0].astype(F32) + dsk_ref[...] * xs_ref[0].astype(F32)
    y = y * _silu(z_ref[0])
    y = _head_rms(y, ng_ref[...], D_INNER // SSD_GROUPS)
    o_ref[0] = x_ref[0] + gate_ref[0] * _dot(y, w_ref[...])


def _odd_out(y, xbc, proj, d_skip, norm_g, w_out, x, gate, tm):
    bsz, t, d = x.shape
    tok = lambda w_, i: pl.BlockSpec((1, tm, w_), lambda b, s: (b, s, i))
    const = lambda a: pl.BlockSpec(a.shape, lambda b, s: (0,) * a.ndim)
    dsk = jnp.repeat(d_skip, SSD_P).reshape(1, D_INNER)
    ng = norm_g.reshape(1, D_INNER)
    return pl.pallas_call(
        _odd_out_kernel,
        grid=(bsz, t // tm),
        in_specs=[tok(D_INNER, 0)] * 4 + [const(dsk), const(ng), const(w_out), tok(d, 0), _bvec_spec(gate)],
        out_specs=tok(d, 0),
        out_shape=jax.ShapeDtypeStruct(x.shape, F32),
        compiler_params=_params("parallel", "parallel"),
        name="odd_out",
    )(*y, xbc, proj, dsk, ng, w_out, x, gate)


FFN_TB = 256
FFN_PAD = GRID_W
FFN_CT = 256


def _ffn_shift_matrix(grid_conv):
    i = np.arange(FFN_TB)[:, None]
    j = np.arange(FFN_TB)[None, :]
    prev, nxt = (j == i - 1), (j == i + 1)
    if grid_conv:
        prev &= (i % GRID_W != 0)
        nxt &= (i % GRID_W != GRID_W - 1)
    return jnp.asarray(np.concatenate([prev, nxt], axis=0), BF16)


def _ffn_conv_block(uc_scr, ul_scr, ur_scr, r0, w9, b, grid_conv):
    acc = b.astype(BF16)
    for dr in ((-1, 0, 1) if grid_conv else (0,)):
        rows = pl.ds(FFN_PAD + r0 + GRID_W * dr, FFN_TB)
        i = 3 * (dr + 1)
        acc = acc + (ul_scr[rows, :] * w9[i:i + 1].astype(BF16) + uc_scr[rows, :] * w9[i + 1:i + 2].astype(BF16)
                     + ur_scr[rows, :] * w9[i + 2:i + 3].astype(BF16))
    return acc


def _ffn_up_kernel(x_ref, g_ref, sc_ref, sh_ref, wa_ref, wg_ref, cwa_ref, cwg_ref, cba_ref, cbg_ref, shift_ref,
                   o_ref, h_scr, *u_scrs, grid_conv):
    t = x_ref.shape[1]
    ct = o_ref.shape[2]

    @pl.when(pl.program_id(1) == 0)
    def _():
        h_scr[...] = _norm_mod(x_ref[0], g_ref[...], sc_ref[0], sh_ref[0]).astype(BF16)
        zeros = jnp.zeros((FFN_PAD, ct), BF16)
        for scr in u_scrs:
            scr[pl.ds(0, FFN_PAD), :] = zeros
            scr[pl.ds(FFN_PAD + t, FFN_PAD), :] = zeros

    halves = ((wa_ref, cwa_ref, cba_ref, u_scrs[0:3]), (wg_ref, cwg_ref, cbg_ref, u_scrs[3:6]))

    def up_block(blk):
        hb = h_scr[pl.ds(blk * FFN_TB, FFN_TB), :]
        for w_ref, _, _, (uc_scr, _, _) in halves:
            uc_scr[pl.ds(FFN_PAD + blk * FFN_TB, FFN_TB), :] = jnp.dot(
                hb, w_ref[...], preferred_element_type=F32).astype(BF16)

    def shift_block(blk):
        rows = pl.ds(FFN_PAD + blk * FFN_TB, FFN_TB)
        for _, _, _, (uc_scr, ul_scr, ur_scr) in halves:
            lr = jnp.dot(shift_ref[...], uc_scr[rows, :], preferred_element_type=F32).astype(BF16)
            ul_scr[rows, :] = lr[:FFN_TB]
            ur_scr[rows, :] = lr[FFN_TB:]

    def conv_block(blk):
        r0 = blk * FFN_TB
        a, g = [_ffn_conv_block(*scrs, r0, cw_ref, cb_ref[...], grid_conv).astype(F32)
                for _, cw_ref, cb_ref, scrs in halves]
        o_ref[0, pl.ds(r0, FFN_TB), :] = (_silu(g) * a).astype(BF16)

    nblk = t // FFN_TB
    for step in range(nblk + 3):
        if step < nblk:
            up_block(step)
        if 0 <= step - 1 < nblk:
            shift_block(step - 1)
        if 0 <= step - 3 < nblk:
            conv_block(step - 3)


def _ffn_up(x, g, sc, sh, w_up, conv_w9, conv_b, grid_conv):
    bsz, t, d = x.shape
    ct = FFN_CT
    nct = D_FF // ct
    cb = conv_b.reshape(1, 2 * D_FF)
    assert t % FFN_TB == 0 and (grid_conv or t == FFN_TB)
    shift = _ffn_shift_matrix(grid_conv)
    return pl.pallas_call(
        functools.partial(_ffn_up_kernel, grid_conv=grid_conv),
        grid=(bsz, nct),
        in_specs=[pl.BlockSpec((1, t, d), lambda b, j: (b, 0, 0)),
                  pl.BlockSpec((1, d), lambda b, j: (0, 0)),
                  _bvec_spec(sc), _bvec_spec(sh),
                  pl.BlockSpec((d, ct), lambda b, j: (0, j)),
                  pl.BlockSpec((d, ct), lambda b, j: (0, nct + j)),
                  pl.BlockSpec((9, ct), lambda b, j: (0, j)),
                  pl.BlockSpec((9, ct), lambda b, j: (0, nct + j)),
                  pl.BlockSpec((1, ct), lambda b, j: (0, j)),
                  pl.BlockSpec((1, ct), lambda b, j: (0, nct + j)),
                  pl.BlockSpec(shift.shape, lambda b, j: (0, 0))],
        out_specs=pl.BlockSpec((1, t, ct), lambda b, j: (b, 0, j)),
        out_shape=jax.ShapeDtypeStruct((bsz, t, D_FF), BF16),
        scratch_shapes=[pltpu.VMEM((t, d), BF16)] + [pltpu.VMEM((t + 2 * FFN_PAD, ct), BF16)] * 6,
        compiler_params=_params("parallel", "arbitrary"),
        name="ffn_up",
    )(x, g.reshape(1, d), sc, sh, w_up, w_up, conv_w9, conv_w9, cb, cb, shift)


def _ffn_down_kernel(a_ref, w_ref, x_ref, gate_ref, fg_ref, o_ref, *, final_norm):
    y = x_ref[0] + gate_ref[0] * jnp.dot(a_ref[0], w_ref[...], preferred_element_type=F32)
    if final_norm:
        y = y * lax.rsqrt(jnp.mean(y * y, axis=-1, keepdims=True) + EPS) * fg_ref[...]
    o_ref[0] = y


def _ffn_down(act, w_down, x, gate, final_g, final_norm, tm):
    bsz, t, d = x.shape
    tok = lambda w_: pl.BlockSpec((1, tm, w_), lambda b, s: (b, s, 0))
    const = lambda a: pl.BlockSpec(a.shape, lambda b, s: (0,) * a.ndim)
    fg = final_g.reshape(1, d)
    return pl.pallas_call(
        functools.partial(_ffn_down_kernel, final_norm=final_norm),
        grid=(bsz, t // tm),
        in_specs=[tok(D_FF), const(w_down), tok(d), _bvec_spec(gate), const(fg)],
        out_specs=tok(d),
        out_shape=jax.ShapeDtypeStruct(x.shape, F32),
        compiler_params=_params("parallel", "parallel"),
        name="ffn_down",
    )(act, w_down, x, gate, fg)


def _rows_to_chunks(a, L):
    bsz, t, ch = a.shape
    return a.reshape(bsz, t // L, L, ch).transpose(0, 1, 3, 2)


def _even_mixer(xc, xl, mods_c, mods_l, norm_g, p):
    (sh1c, sc1c, g1c), (sh1l, sc1l, g1l) = mods_c, mods_l
    w_in = jnp.concatenate([p["w_in"], p["a1"][0], p["a1"][1],
                            jnp.zeros((D_MODEL, EVEN_N - EVEN_IN - 2 * GLA_RANK), F32)], axis=1).astype(BF16)
    mq_off = GLA_QK * 2 + GLA_V * 2
    projs, qks = {}, {}
    for s, (xs, sc1, sh1, tm) in (("c", (xc, sc1c, sh1c, 256)), ("l", (xl, sc1l, sh1l, 512))):
        projs[s], qks[s] = _norm_proj(xs, norm_g, sc1, sh1, w_in, tm, mq_off, p["conv_w"], p["conv_b"])
    gates_r = {s: _rows_to_chunks(projs[s][:, :, EVEN_IN - 16:EVEN_IN], ML_CHUNK) for s in projs}
    bsz = xl.shape[0]

    a2p = jnp.stack([jnp.zeros((LANES, GLA_QK), F32).at[16 + 16 * d:32 + 16 * d].set(p["a2"][d])
                     for d in range(2)]).astype(BF16)
    ab = p["ab"].reshape(2, 1, GLA_QK)
    gla_state = jnp.zeros((bsz, 2, 2, 2 * GLA_DV, LANES), F32)
    ml_state = (jnp.zeros((bsz, 2, ML_HEADS, ML_D, 2 * ML_D), F32), jnp.zeros((bsz, 2, ML_HEADS, 1, LANES), F32))
    o_gla, h_ml = {}, {}
    for s in ("c", "l"):
        of, ob, gla_state = _gla_scan(projs[s], a2p, ab, gla_state)
        o_gla[s] = (of, ob)
        hf, hb, ml_state = _mlstm_scan(qks[s], projs[s], gates_r[s], p["gate_b"], ml_state)
        h_ml[s] = (hf, hb)

    w_out = p["w_out"].astype(BF16)
    xc = _even_out(o_gla["c"], h_ml["c"], projs["c"], p["gla_norm_g"], p["ml_norm_g"], w_out, xc, g1c, 256)
    xl = _even_out(o_gla["l"], h_ml["l"], projs["l"], p["gla_norm_g"], p["ml_norm_g"], w_out, xl, g1l, 512)
    return xc, xl


def _odd_mixer_last(xc, xl, mods_c, mods_l, norm_g, p):
    (sh1c, sc1c, _), (sh1l, sc1l, g1l) = mods_c, mods_l
    w_in = jnp.concatenate([p["w_in"], jnp.zeros((D_MODEL, ODD_N - ODD_IN), F32)], axis=1).astype(BF16)
    projs, xbcs = {}, {}
    for s, (xs, sc1, sh1, tm) in (("c", (xc, sc1c, sh1c, 256)), ("l", (xl, sc1l, sh1l, 512))):
        projs[s], xbcs[s] = _norm_proj(xs, norm_g, sc1, sh1, w_in, tm, D_INNER, p["conv_w"], p["conv_b"])
    dt_rs = {s: _rows_to_chunks(projs[s][:, :, ODD_IN - 2 * SSD_HEADS:ODD_IN], SSD_CHUNK) for s in projs}
    bsz = xl.shape[0]
    state = jnp.zeros((bsz, 2, SSD_GROUPS, SSD_STATE, SSD_HPG * SSD_P), F32)
    _, _, state = _ssd_scan(xbcs["c"], projs["c"], dt_rs["c"], p["dt_bias"], p["a_log"], state, False)
    yf, yb, _ = _ssd_scan(xbcs["l"], projs["l"], dt_rs["l"], p["dt_bias"], p["a_log"], state, True)
    return _odd_out((yf, yb), xbcs["l"], projs["l"], p["d_skip"], p["norm_g"], p["w_out"].astype(BF16),
                    xl, g1l, 512)


def kernel(x, c, ctx, c_ctx, mod_w, mod_b, norm_mix_g, norm_ffn_g, final_norm_g, ffn_w_up, ffn_conv_w, ffn_conv_b, ffn_w_down, even_w_in, gla_a1, gla_a2, gla_ab, ml_conv_w, ml_conv_b, ml_gate_b, gla_norm_g, ml_norm_g, even_w_out, ssd_w_in, ssd_conv_w, ssd_conv_b, ssd_dt_bias, ssd_a_log, ssd_d, ssd_norm_g, ssd_w_out):
    depth = mod_w.shape[0]
    bsz = x.shape[0]
    assert depth == 2 and x.shape[1] % (GRID_W * SUBLANES) == 0
    assert bsz % GLA_NB == 0 and bsz % ML_NB == 0 and bsz % SSD_NB == 0
    rows = -(-(bsz + 1) // SUBLANES) * SUBLANES
    cc = jnp.zeros((rows, D_MODEL), F32).at[:bsz].set(c).at[bsz].set(c_ctx)
    mod = _modulation(cc, mod_w, mod_b)

    def mods(layer, lo, hi):
        return [mod[layer, lo:hi, i * D_MODEL:(i + 1) * D_MODEL][:, None, :] for i in range(6)]

    xl, xc = x, ctx
    for layer in range(depth):
        last = layer == depth - 1
        sh1l, sc1l, g1l, sh2l, sc2l, g2l = mods(layer, 0, bsz)
        sh1c, sc1c, g1c, sh2c, sc2c, g2c = mods(layer, bsz, bsz + 1)
        if layer == 0:
            p = dict(w_in=even_w_in[0], a1=gla_a1[0], a2=gla_a2[0], ab=gla_ab[0], conv_w=ml_conv_w[0],
                     conv_b=ml_conv_b[0], gate_b=ml_gate_b[0], gla_norm_g=gla_norm_g[0],
                     ml_norm_g=ml_norm_g[0], w_out=even_w_out[0])
            xc, xl = _even_mixer(xc, xl, (sh1c, sc1c, g1c), (sh1l, sc1l, g1l), norm_mix_g[layer], p)
        else:
            p = dict(w_in=ssd_w_in[0], conv_w=ssd_conv_w[0], conv_b=ssd_conv_b[0], dt_bias=ssd_dt_bias[0],
                     a_log=ssd_a_log[0], d_skip=ssd_d[0], norm_g=ssd_norm_g[0], w_out=ssd_w_out[0])
            xl = _odd_mixer_last(xc, xl, (sh1c, sc1c, g1c), (sh1l, sc1l, g1l), norm_mix_g[layer], p)
        w_up = ffn_w_up[layer].astype(BF16)
        w_down = ffn_w_down[layer].astype(BF16)
        cw9 = ffn_conv_w[layer].reshape(9, 2 * D_FF)
        act = _ffn_up(xl, norm_ffn_g[layer], sc2l, sh2l, w_up, cw9, ffn_conv_b[layer], True)
        xl = _ffn_down(act, w_down, xl, g2l, final_norm_g, last, 1024)
        if not last:
            act = _ffn_up(xc, norm_ffn_g[layer], sc2c, sh2c, w_up, cw9, ffn_conv_b[layer], False)
            xc = _ffn_down(act, w_down, xc, g2c, final_norm_g, False, 256)
    return xl
```
